```python
import jax, jax.numpy as jnp
from jax import lax
import numpy as np

D_MODEL = 2048
BATCH = 4
SEQ = 2048
DEPTH = 2
DEC_BATCH = 32
DEC_SEQ = 4
PAST_LEN = 8192
PAGE_SIZE = 128

N_MIXERS = 2
N_ATTN_LAYERS = (DEPTH + 1) // 2
N_CONV_LAYERS = DEPTH // 2
N_HEADS = 16
HEAD_DIM = D_MODEL // N_HEADS
MOBA_BLOCK = 256
MOBA_TOPK = 3
Q_CHUNK = 16
CONV_WIDTH = 3
PEER_HEADS = 8
PEER_NKEYS = 128
PEER_EXPERTS = PEER_NKEYS * PEER_NKEYS
PEER_TOPK = 16
PEER_QDIM = 256
PEER_HALF = PEER_QDIM // 2
PEER_TOK_CHUNK = 128
LN_EPS = 1e-5
DEEPNORM_ALPHA = (2.0 * DEPTH) ** 0.25
DEEPNORM_BETA = (8.0 * DEPTH) ** -0.25

kernel_name = 'hybrid_moba_shortconv_peer_step'


def layer_norm(x, g, b):
    xf = x.astype(jnp.float32)
    mu = jnp.mean(xf, axis=-1, keepdims=True)
    var = jnp.mean(jnp.square(xf - mu), axis=-1, keepdims=True)
    return ((xf - mu) * lax.rsqrt(var + LN_EPS)).astype(x.dtype) * g + b


def split_heads(t):
    n, s, _ = t.shape
    return t.reshape(n, s, N_HEADS, HEAD_DIM).transpose(0, 2, 1, 3)


def merge_heads(o):
    n, _, s, _ = o.shape
    return o.transpose(0, 2, 1, 3).reshape(n, s, D_MODEL)


def moba_prompt(x, w_qkv, w_o):
    b, s, _ = x.shape
    q, k, v = [split_heads(t) for t in jnp.split(x @ w_qkv, 3, axis=-1)]
    n_full = s // MOBA_BLOCK
    n_blk = -(-s // MOBA_BLOCK)
    pad = n_blk * MOBA_BLOCK - s
    padw = ((0, 0), (0, 0), (0, pad), (0, 0))
    k_blocks = jnp.pad(k, padw).reshape(b, N_HEADS, n_blk, MOBA_BLOCK, HEAD_DIM)
    v_blocks = jnp.pad(v, padw).reshape(b, N_HEADS, n_blk, MOBA_BLOCK, HEAD_DIM)
    n_sel = min(MOBA_TOPK, n_full)
    scale = HEAD_DIM ** -0.5
    if n_sel > 0:
        k_mean = jnp.mean(k_blocks[:, :, :n_full].astype(jnp.float32), axis=3)
    bi = jnp.arange(b)[:, None, None, None]
    hi = jnp.arange(N_HEADS)[None, :, None, None]

    def chunk(c):
        t0 = c * Q_CHUNK
        qb = t0 // MOBA_BLOCK
        qc = lax.dynamic_slice_in_dim(q, t0, Q_CHUNK, axis=2)
        qpos = t0 + jnp.arange(Q_CHUNK)
        own_k = lax.dynamic_index_in_dim(k_blocks, qb, axis=2, keepdims=False)
        own_v = lax.dynamic_index_in_dim(v_blocks, qb, axis=2, keepdims=False)
        kpos = qb * MOBA_BLOCK + jnp.arange(MOBA_BLOCK)
        s_own = jnp.einsum('bhqd,bhkd->bhqk', qc, own_k).astype(jnp.float32) * scale
        s_own = jnp.where(kpos[None, :] <= qpos[:, None], s_own, -jnp.inf)
        if n_sel == 0:
            p = jax.nn.softmax(s_own, axis=-1).astype(own_v.dtype)
            return jnp.einsum('bhqk,bhkd->bhqd', p, own_v)
        gate = jnp.einsum('bhqd,bhnd->bhqn', qc.astype(jnp.float32), k_mean)
        gate = jnp.where(jnp.arange(n_full) < qb, gate, -jnp.inf)
        _, idx = lax.top_k(gate, n_sel)
        valid = idx < qb
        sel_k = k_blocks[bi, hi, idx].reshape(b, N_HEADS, Q_CHUNK, n_sel * MOBA_BLOCK, HEAD_DIM)
        sel_v = v_blocks[bi, hi, idx].reshape(b, N_HEADS, Q_CHUNK, n_sel * MOBA_BLOCK, HEAD_DIM)
        s_past = jnp.einsum('bhqd,bhqkd->bhqk', qc, sel_k).astype(jnp.float32) * scale
        s_past = jnp.where(jnp.repeat(valid, MOBA_BLOCK, axis=-1), s_past, -jnp.inf)
        p = jax.nn.softmax(jnp.concatenate([s_past, s_own], axis=-1), axis=-1).astype(v.dtype)
        n_p = n_sel * MOBA_BLOCK
        return (jnp.einsum('bhqk,bhqkd->bhqd', p[..., :n_p], sel_v)
                + jnp.einsum('bhqk,bhkd->bhqd', p[..., n_p:], own_v))

    o = lax.map(chunk, jnp.arange(s // Q_CHUNK))
    o = o.transpose(1, 0, 3, 2, 4).reshape(b, s, D_MODEL)
    to_pages = lambda t: t.reshape(b, N_HEADS, s // PAGE_SIZE, PAGE_SIZE, HEAD_DIM).transpose(0, 2, 1, 3, 4)
    return o @ w_o, to_pages(k), to_pages(v)


def moba_sample(x, cache_k, cache_v, k_page_sum, page_table, li, w_qkv, w_o):
    n, t, _ = x.shape
    q, k, v = [split_heads(a) for a in jnp.split(x @ w_qkv, 3, axis=-1)]
    scale = HEAD_DIM ** -0.5
    ppb = MOBA_BLOCK // PAGE_SIZE
    cb = PAST_LEN // MOBA_BLOCK
    n_sel = min(MOBA_TOPK, cb)
    n_past_pages = PAST_LEN // PAGE_SIZE
    own_first = cb * ppb
    n_own = n_past_pages - own_first
    own_pages = page_table[:, own_first:n_past_pages]
    ck = cache_k[own_pages, li].transpose(0, 2, 1, 3, 4).reshape(n, N_HEADS, n_own * PAGE_SIZE, HEAD_DIM)
    cv = cache_v[own_pages, li].transpose(0, 2, 1, 3, 4).reshape(n, N_HEADS, n_own * PAGE_SIZE, HEAD_DIM)
    own_k = jnp.concatenate([ck, k], axis=2)
    own_v = jnp.concatenate([cv, v], axis=2)
    own_mask = jnp.concatenate([jnp.ones((t, n_own * PAGE_SIZE), bool), jnp.tril(jnp.ones((t, t), bool))], axis=1)
    s_own = jnp.einsum('bhtd,bhkd->bhtk', q, own_k).astype(jnp.float32) * scale
    s_own = jnp.where(own_mask, s_own, -jnp.inf)
    if n_sel == 0:
        p = jax.nn.softmax(s_own, axis=-1).astype(own_v.dtype)
        o = jnp.einsum('bhtk,bhkd->bhtd', p, own_v)
        return merge_heads(o) @ w_o, k, v
    blk_pages = page_table[:, :cb * ppb]
    k_mean = k_page_sum[blk_pages, li].reshape(n, cb, ppb, N_HEADS, HEAD_DIM).sum(axis=2) / MOBA_BLOCK
    gate = jnp.einsum('bhtd,bnhd->bhtn', q.astype(jnp.float32), k_mean)
    _, idx = lax.top_k(gate, n_sel)
    bi = jnp.arange(n)[:, None, None, None, None]
    hi = jnp.arange(N_HEADS)[None, :, None, None, None]
    logical = idx[..., None] * ppb + jnp.arange(ppb)
    pages = page_table[bi, logical]
    sel_k = cache_k[pages, li, hi].reshape(n, N_HEADS, t, n_sel * MOBA_BLOCK, HEAD_DIM)
    sel_v = cache_v[pages, li, hi].reshape(n, N_HEADS, t, n_sel * MOBA_BLOCK, HEAD_DIM)
    s_past = jnp.einsum('bhtd,bhtkd->bhtk', q, sel_k).astype(jnp.float32) * scale
    p = jax.nn.softmax(jnp.concatenate([s_past, s_own], axis=-1), axis=-1).astype(v.dtype)
    n_p = n_sel * MOBA_BLOCK
    o = (jnp.einsum('bhtk,bhtkd->bhtd', p[..., :n_p], sel_v)
         + jnp.einsum('bhtk,bhkd->bhtd', p[..., n_p:], own_v))
    return merge_heads(o) @ w_o, k, v


def short_conv(x, prev, w_in, conv_w, w_out):
    t = x.shape[1]
    bg, cg, h = jnp.split(x @ w_in, 3, axis=-1)
    z = cg * h
    zp = jnp.concatenate([prev, z], axis=1)
    y = sum(conv_w[j] * zp[:, j:j + t] for j in range(CONV_WIDTH))
    return (bg * y) @ w_out, zp[:, -(CONV_WIDTH - 1):]


def peer(x, w_q, sub_keys, expert_u, expert_v):
    shape = x.shape
    xf = x.reshape(-1, D_MODEL)
    n = xf.shape[0]
    n_pad = -(-n // PEER_TOK_CHUNK) * PEER_TOK_CHUNK
    xc = jnp.pad(xf, ((0, n_pad - n), (0, 0))).reshape(-1, PEER_TOK_CHUNK, D_MODEL)

    def chunk(xt):
        c = xt.shape[0]
        q = (xt @ w_q).reshape(c, PEER_HEADS, 2, PEER_HALF)
        s = jnp.einsum('chpd,hpkd->chpk', q, sub_keys).astype(jnp.float32)
        sv, si = lax.top_k(s, PEER_TOPK)
        cand = (sv[:, :, 0, :, None] + sv[:, :, 1, None, :]).reshape(c, PEER_HEADS, PEER_TOPK * PEER_TOPK)
        fv, fi = lax.top_k(cand, PEER_TOPK)
        e1 = jnp.take_along_axis(si[:, :, 0], fi // PEER_TOPK, axis=-1)
        e2 = jnp.take_along_axis(si[:, :, 1], fi % PEER_TOPK, axis=-1)
        e = e1 * PEER_NKEYS + e2
        g = jax.nn.softmax(fv, axis=-1)
        a = jax.nn.gelu(jnp.einsum('cd,chkd->chk', xt, expert_u[e]).astype(jnp.float32), approximate=False)
        return jnp.einsum('chk,chkd->cd', (g * a).astype(xt.dtype), expert_v[e])

    out = lax.map(chunk, xc).reshape(n_pad, D_MODEL)[:n]
    return out.reshape(shape)


def setup_inputs(seed: int = 0) -> dict:
    key = jax.random.key(seed)
    ks = jax.random.split(key, 24)
    f32 = jnp.float32
    nrm = lambda k, shape, scale: jax.random.normal(k, shape, f32) * scale
    n_pages = PAST_LEN // PAGE_SIZE
    n_used = DEC_BATCH * n_pages
    n_pool = n_used + max(1, n_used // 4)
    perm = jax.random.permutation(ks[5], n_pool)
    page_table = perm[:n_used].reshape(DEC_BATCH, n_pages).astype(jnp.int32)
    d = D_MODEL
    return {
        'x_prompt': nrm(ks[0], (BATCH, SEQ, d), 1.0),
        'x_sample': nrm(ks[1], (DEC_BATCH, DEC_SEQ, d), 1.0),
        'cache_k': nrm(ks[2], (n_pool, N_ATTN_LAYERS, N_HEADS, PAGE_SIZE, HEAD_DIM), 1.0),
        'cache_v': nrm(ks[3], (n_pool, N_ATTN_LAYERS, N_HEADS, PAGE_SIZE, HEAD_DIM), 1.0),
        'state_conv': nrm(ks[4], (N_CONV_LAYERS, DEC_BATCH, CONV_WIDTH - 1, d), 1.0),
        'page_table': page_table,
        'attn_w_qkv': nrm(ks[6], (N_ATTN_LAYERS, d, 3 * d), d ** -0.5),
        'attn_w_o': nrm(ks[7], (N_ATTN_LAYERS, d, d), d ** -0.5 * DEEPNORM_BETA),
        'conv_w_in': nrm(ks[8], (N_CONV_LAYERS, d, 3 * d), d ** -0.5),
        'conv_w': nrm(ks[9], (N_CONV_LAYERS, CONV_WIDTH, d), CONV_WIDTH ** -0.5),
        'conv_w_out': nrm(ks[10], (N_CONV_LAYERS, d, d), d ** -0.5 * DEEPNORM_BETA),
        'ln_mix_g': 1.0 + nrm(ks[11], (DEPTH, d), 0.02),
        'ln_mix_b': nrm(ks[12], (DEPTH, d), 0.02),
        'ln_ffn_g': 1.0 + nrm(ks[13], (DEPTH, d), 0.02),
        'ln_ffn_b': nrm(ks[14], (DEPTH, d), 0.02),
        'peer_w_q': nrm(ks[15], (DEPTH, d, PEER_HEADS * PEER_QDIM), d ** -0.5),
        'peer_sub_keys': nrm(ks[16], (DEPTH, PEER_HEADS, 2, PEER_NKEYS, PEER_HALF), PEER_HALF ** -0.5),
        'peer_u': nrm(ks[17], (DEPTH, PEER_EXPERTS, d), d ** -0.5),
        'peer_v': nrm(ks[18], (DEPTH, PEER_EXPERTS, d), (PEER_HEADS * PEER_TOPK) ** -0.5 * DEEPNORM_BETA),
    }


def reference(x_prompt, x_sample, cache_k, cache_v, state_conv, page_table, attn_w_qkv, attn_w_o,
              conv_w_in, conv_w, conv_w_out, ln_mix_g, ln_mix_b, ln_ffn_g, ln_ffn_b,
              peer_w_q, peer_sub_keys, peer_u, peer_v):
    k_page_sum = jnp.sum(cache_k, axis=3, dtype=jnp.float32)
    hp, hs = x_prompt, x_sample
    kp_l, vp_l, ks_l, vs_l, cp_l, cs_l = [], [], [], [], [], []
    for i in range(DEPTH):
        j = i // N_MIXERS
        if i % N_MIXERS == 0:
            mp, kp, vp = moba_prompt(hp, attn_w_qkv[j], attn_w_o[j])
            ms, kss, vss = moba_sample(hs, cache_k, cache_v, k_page_sum, page_table, j, attn_w_qkv[j], attn_w_o[j])
            kp_l.append(kp); vp_l.append(vp); ks_l.append(kss); vs_l.append(vss)
        else:
            zeros = jnp.zeros((hp.shape[0], CONV_WIDTH - 1, D_MODEL), hp.dtype)
            mp, cp = short_conv(hp, zeros, conv_w_in[j], conv_w[j], conv_w_out[j])
            ms, cs = short_conv(hs, state_conv[j], conv_w_in[j], conv_w[j], conv_w_out[j])
            cp_l.append(cp); cs_l.append(cs)
        hp = layer_norm(DEEPNORM_ALPHA * hp + mp, ln_mix_g[i], ln_mix_b[i])
        hs = layer_norm(DEEPNORM_ALPHA * hs + ms, ln_mix_g[i], ln_mix_b[i])
        hp = layer_norm(DEEPNORM_ALPHA * hp + peer(hp, peer_w_q[i], peer_sub_keys[i], peer_u[i], peer_v[i]), ln_ffn_g[i], ln_ffn_b[i])
        hs = layer_norm(DEEPNORM_ALPHA * hs + peer(hs, peer_w_q[i], peer_sub_keys[i], peer_u[i], peer_v[i]), ln_ffn_g[i], ln_ffn_b[i])
    new_k_prompt = jnp.stack(kp_l, axis=2)
    new_v_prompt = jnp.stack(vp_l, axis=2)
    new_k_sample = jnp.stack(ks_l, axis=1)
    new_v_sample = jnp.stack(vs_l, axis=1)
    new_conv_prompt = jnp.stack(cp_l, axis=0)
    new_conv_sample = jnp.stack(cs_l, axis=0)
    return (hp, hs, new_k_prompt, new_v_prompt, new_k_sample, new_v_sample, new_conv_prompt, new_conv_sample)
```

```python
import functools

import jax
import jax.numpy as jnp
from jax import lax
from jax.experimental import pallas as pl
from jax.experimental.pallas import tpu as pltpu

F32 = jnp.float32
BF16 = jnp.bfloat16

D_MODEL = 2048
N_HEADS = 16
HEAD_DIM = 128
PAGE_SIZE = 128
MOBA_BLOCK = 256
MOBA_TOPK = 3
PAGES_PER_BLOCK = MOBA_BLOCK // PAGE_SIZE
CONV_WIDTH = 3
PEER_HEADS = 8
PEER_NKEYS = 128
PEER_TOPK = 16
PEER_HALF = 128
LN_EPS = 1e-5
DEPTH = 2
DEEPNORM_ALPHA = (2.0 * DEPTH) ** 0.25
ATTN_SCALE = HEAD_DIM ** -0.5
NEG_INF = float("-inf")

V7X_LANES = 128
V7X_SUBLANES = 8
V7X_VMEM_LIMIT_BYTES = 60 * 1024 * 1024

_CONTRACT_LAST = (((1,), (1,)), ((), ()))
_CONTRACT_FIRST = (((0,), (0,)), ((), ()))
_BATCH_CONTRACT_LAST = (((2,), (2,)), ((0,), (0,)))
_BATCH_MATMUL = (((2,), (1,)), ((0,), (0,)))


def _params(n_axes):
    return pltpu.CompilerParams(dimension_semantics=("arbitrary",) * n_axes,
                                vmem_limit_bytes=V7X_VMEM_LIMIT_BYTES)


def _dot(a, b):
    return jnp.dot(a, b, preferred_element_type=F32)


def _dot_nt(a, b):
    return lax.dot_general(a, b, _CONTRACT_LAST, preferred_element_type=F32)


def _split_bf16(x):
    hi = x.astype(BF16)
    lo = (x - hi.astype(F32)).astype(BF16)
    return hi, lo


def _layer_norm(x, g, b):
    mu = jnp.mean(x, axis=-1, keepdims=True)
    xc = x - mu
    var = jnp.mean(xc * xc, axis=-1, keepdims=True)
    return xc * lax.rsqrt(var + LN_EPS) * g + b


def _mm_kernel(x_ref, w_ref, o_ref):
    o_ref[...] = _dot(x_ref[...].astype(BF16), w_ref[...])


def _mm(x, w, tm, tn):
    m, k = x.shape
    n = w.shape[1]
    return pl.pallas_call(
        _mm_kernel,
        grid=(n // tn, m // tm),
        in_specs=[pl.BlockSpec((tm, k), lambda j, i: (i, 0)),
                  pl.BlockSpec((k, tn), lambda j, i: (0, j))],
        out_specs=pl.BlockSpec((tm, tn), lambda j, i: (i, j)),
        out_shape=jax.ShapeDtypeStruct((m, n), F32),
        compiler_params=_params(2),
        name="proj",
    )(x, w)


def _mm_res_ln_kernel(x_ref, w_ref, h_ref, g_ref, b_ref, o_ref):
    y = _dot(x_ref[...].astype(BF16), w_ref[...])
    o_ref[...] = _layer_norm(DEEPNORM_ALPHA * h_ref[...] + y, g_ref[...], b_ref[...])


def _mm_res_ln(x, w, h, g, b, tm):
    m, k = x.shape
    d = w.shape[1]
    row = lambda i: (i, 0)
    fixed = lambda i: (0, 0)
    return pl.pallas_call(
        _mm_res_ln_kernel,
        grid=(m // tm,),
        in_specs=[pl.BlockSpec((tm, k), row), pl.BlockSpec((k, d), fixed),
                  pl.BlockSpec((tm, d), row), pl.BlockSpec((1, d), fixed), pl.BlockSpec((1, d), fixed)],
        out_specs=pl.BlockSpec((tm, d), row),
        out_shape=jax.ShapeDtypeStruct((m, d), F32),
        compiler_params=_params(1),
        name="out_proj_ln",
    )(x, w, h, g, b)


def _conv_in_kernel(x_ref, wb_ref, wc_ref, wh_ref, bg_ref, z_ref):
    xb = x_ref[...].astype(BF16)
    bg_ref[...] = _dot(xb, wb_ref[...])
    z_ref[...] = _dot(xb, wc_ref[...]) * _dot(xb, wh_ref[...])


def _conv_in(x, w_in, tm, tn):
    m, k = x.shape
    d = w_in.shape[1] // 3
    nb = d // tn
    xs = pl.BlockSpec((tm, k), lambda j, i: (i, 0))
    ws = [pl.BlockSpec((k, tn), functools.partial(lambda j, i, off: (0, j + off), off=part * nb))
          for part in range(3)]
    os_ = pl.BlockSpec((tm, tn), lambda j, i: (i, j))
    return pl.pallas_call(
        _conv_in_kernel,
        grid=(nb, m // tm),
        in_specs=[xs] + ws,
        out_specs=[os_, os_],
        out_shape=[jax.ShapeDtypeStruct((m, d), F32)] * 2,
        compiler_params=_params(2),
        name="conv_in",
    )(x, w_in, w_in, w_in)


def _conv_out_prompt_kernel(bg_ref, z_ref, zp_ref, cw_ref, w_ref, h_ref, g_ref, b_ref, o_ref, *, tiles_per_seq):
    i = pl.program_id(0)
    z = z_ref[...]
    seq_start = (i % tiles_per_seq) == 0
    zp = jnp.where(seq_start, 0.0, zp_ref[...])
    row = lax.broadcasted_iota(jnp.int32, z.shape, 0)
    last = V7X_SUBLANES - 1
    z1 = jnp.where(row == 0, zp[last:last + 1, :], pltpu.roll(z, 1, 0))
    z2 = jnp.where(row == 0, zp[last - 1:last, :],
                   jnp.where(row == 1, zp[last:last + 1, :], pltpu.roll(z, 2, 0)))
    cw = cw_ref[...]
    y = cw[0:1, :] * z2 + cw[1:2, :] * z1 + cw[2:3, :] * z
    u = (bg_ref[...] * y).astype(BF16)
    o_ref[...] = _layer_norm(DEEPNORM_ALPHA * h_ref[...] + _dot(u, w_ref[...]), g_ref[...], b_ref[...])


def _conv_out_prompt(bg, z, conv_w, w_out, h, g, b, tm, seq_len):
    m, d = z.shape
    row = lambda i: (i, 0)
    fixed = lambda i: (0, 0)
    halo = lambda i: (jnp.maximum(i * (tm // V7X_SUBLANES) - 1, 0), 0)
    return pl.pallas_call(
        functools.partial(_conv_out_prompt_kernel, tiles_per_seq=seq_len // tm),
        grid=(m // tm,),
        in_specs=[pl.BlockSpec((tm, d), row), pl.BlockSpec((tm, d), row),
                  pl.BlockSpec((V7X_SUBLANES, d), halo), pl.BlockSpec((CONV_WIDTH, d), fixed),
                  pl.BlockSpec((d, d), fixed), pl.BlockSpec((tm, d), row),
                  pl.BlockSpec((1, d), fixed), pl.BlockSpec((1, d), fixed)],
        out_specs=pl.BlockSpec((tm, d), row),
        out_shape=jax.ShapeDtypeStruct((m, d), F32),
        compiler_params=_params(1),
        name="conv_out_prompt",
    )(bg, z, z, conv_w, w_out, h, g, b)


def _conv_sample_kernel(bg_ref, z_ref, st_ref, cw_ref, u_ref, ns_ref, *, t):
    cw = cw_ref[...]
    zp = [st_ref[j] for j in range(CONV_WIDTH - 1)] + [z_ref[j] for j in range(t)]
    for j in range(t):
        y = cw[0:1, :] * zp[j] + cw[1:2, :] * zp[j + 1] + cw[2:3, :] * zp[j + 2]
        u_ref[j] = bg_ref[j] * y
    for j in range(CONV_WIDTH - 1):
        ns_ref[j] = zp[t + j]


def _conv_sample(bg, z, state, conv_w):
    t, n, d = z.shape
    return pl.pallas_call(
        functools.partial(_conv_sample_kernel, t=t),
        out_shape=[jax.ShapeDtypeStruct((t, n, d), F32), jax.ShapeDtypeStruct((CONV_WIDTH - 1, n, d), F32)],
        name="conv_sample",
    )(bg, z, state, conv_w)


def _topk_mask(g, n_valid, lane, n_cand):
    rank = jnp.zeros(g.shape, F32)
    for c in range(n_cand):
        col = g[..., c:c + 1]
        beats = (col > g) | ((col == g) & (c < lane))
        rank = rank + jnp.where(beats & (c < n_valid), 1.0, 0.0)
    return (lane < n_valid) & (rank < MOBA_TOPK)


def _moba_prompt_kernel(q_ref, k_ref, v_ref, o_ref, kp_ref, vp_ref, km_ref, m_ref, l_ref, acc_ref, *, n_blk):
    qb = pl.program_id(2)

    @pl.when(qb == 0)
    def _():
        k = k_ref[...]
        kp_ref[...] = k.reshape(kp_ref.shape)
        vp_ref[...] = v_ref[...].reshape(vp_ref.shape)
        km_ref[...] = jnp.zeros(km_ref.shape, F32)
        for n in range(n_blk):
            km_ref[n:n + 1, :] = jnp.mean(k[n * MOBA_BLOCK:(n + 1) * MOBA_BLOCK, :], axis=0, keepdims=True)

    q_hi, q_lo = _split_bf16(q_ref[...])
    km_hi, km_lo = _split_bf16(km_ref[...])
    gate = _dot_nt(q_hi, km_hi) + _dot_nt(q_lo, km_hi) + _dot_nt(q_hi, km_lo)
    lane = lax.broadcasted_iota(jnp.int32, gate.shape, 1)
    sel = jnp.where(_topk_mask(gate, qb, lane, n_blk - 1), 1.0, 0.0)

    own = pl.ds(pl.multiple_of(qb * MOBA_BLOCK, MOBA_BLOCK), MOBA_BLOCK)
    s = _dot_nt(q_hi, k_ref[own, :].astype(BF16)) * ATTN_SCALE
    r_id = lax.broadcasted_iota(jnp.int32, s.shape, 0)
    c_id = lax.broadcasted_iota(jnp.int32, s.shape, 1)
    s = jnp.where(c_id <= r_id, s, NEG_INF)
    m0 = jnp.max(s, axis=-1, keepdims=True)
    p = jnp.exp(s - m0)
    m_ref[...] = m0
    l_ref[...] = jnp.sum(p, axis=-1, keepdims=True)
    acc_ref[...] = _dot(p.astype(BF16), v_ref[own, :].astype(BF16))

    for n in range(n_blk - 1):
        @pl.when(n < qb)
        def _(n=n):
            blk = slice(n * MOBA_BLOCK, (n + 1) * MOBA_BLOCK)
            s = _dot_nt(q_hi, k_ref[blk, :].astype(BF16)) * ATTN_SCALE
            s = jnp.where(sel[:, n:n + 1] > 0.5, s, NEG_INF)
            m_old = m_ref[...]
            m_new = jnp.maximum(m_old, jnp.max(s, axis=-1, keepdims=True))
            alpha = jnp.exp(m_old - m_new)
            p = jnp.exp(s - m_new)
            l_ref[...] = alpha * l_ref[...] + jnp.sum(p, axis=-1, keepdims=True)
            acc_ref[...] = alpha * acc_ref[...] + _dot(p.astype(BF16), v_ref[blk, :].astype(BF16))
            m_ref[...] = m_new

    o_ref[...] = (acc_ref[...] / l_ref[...]).astype(o_ref.dtype)


def _moba_prompt(qkv, batch, seq):
    n_blk = seq // MOBA_BLOCK
    n_pages = seq // PAGE_SIZE
    hd = HEAD_DIM
    q_spec = pl.BlockSpec((MOBA_BLOCK, hd), lambda b, h, qb: (b * n_blk + qb, h))
    k_spec = pl.BlockSpec((seq, hd), lambda b, h, qb: (b, N_HEADS + h))
    v_spec = pl.BlockSpec((seq, hd), lambda b, h, qb: (b, 2 * N_HEADS + h))
    o_spec = pl.BlockSpec((MOBA_BLOCK, hd), lambda b, h, qb: (b * n_blk + qb, h))
    page_spec = pl.BlockSpec((None, n_pages, None, None, PAGE_SIZE, hd), lambda b, h, qb: (b, 0, 0, h, 0, 0))
    page_shape = jax.ShapeDtypeStruct((batch, n_pages, 1, N_HEADS, PAGE_SIZE, hd), F32)
    return pl.pallas_call(
        functools.partial(_moba_prompt_kernel, n_blk=n_blk),
        grid=(batch, N_HEADS, n_blk),
        in_specs=[q_spec, k_spec, v_spec],
        out_specs=[o_spec, page_spec, page_spec],
        out_shape=[jax.ShapeDtypeStruct((batch * seq, D_MODEL), BF16), page_shape, page_shape],
        scratch_shapes=[pltpu.VMEM((V7X_LANES, hd), F32), pltpu.VMEM((MOBA_BLOCK, 1), F32),
                        pltpu.VMEM((MOBA_BLOCK, 1), F32), pltpu.VMEM((MOBA_BLOCK, hd), F32)],
        compiler_params=_params(3),
        name="moba_prompt",
    )(qkv, qkv, qkv)


def _page_mean_kernel(pt_ref, ka_ref, kb_ref, o_ref):
    del pt_ref
    o_ref[...] = (jnp.sum(ka_ref[...], axis=1) + jnp.sum(kb_ref[...], axis=1)) / MOBA_BLOCK


def _block_key_means(cache_k, page_table, layer, n_blocks):
    n = page_table.shape[0]
    hd = HEAD_DIM
    page = lambda which: pl.BlockSpec(
        (None, None, N_HEADS, PAGE_SIZE, hd),
        lambda i, j, pt: (pt[i, PAGES_PER_BLOCK * j + which], layer, 0, 0, 0))
    return pl.pallas_call(
        _page_mean_kernel,
        grid_spec=pltpu.PrefetchScalarGridSpec(
            num_scalar_prefetch=1, grid=(n, n_blocks),
            in_specs=[page(0), page(1)],
            out_specs=pl.BlockSpec((None, None, N_HEADS, hd), lambda i, j, pt: (i, j, 0, 0))),
        out_shape=jax.ShapeDtypeStruct((n, n_blocks, N_HEADS, hd), F32),
        compiler_params=_params(2),
        name="block_key_means",
    )(page_table, cache_k, cache_k)


def _sample_select_kernel(q_ref, km_ref, sel_ref, *, n_blocks):
    q_hi, q_lo = _split_bf16(q_ref[...])
    km_hi, km_lo = _split_bf16(km_ref[...])
    bd = functools.partial(lax.dot_general, dimension_numbers=_BATCH_CONTRACT_LAST, preferred_element_type=F32)
    gate = bd(q_hi, km_hi) + bd(q_lo, km_hi) + bd(q_hi, km_lo)
    lane = lax.broadcasted_iota(jnp.int32, gate.shape, 2)
    sel_ref[...] = jnp.where(_topk_mask(gate, n_blocks, lane, n_blocks), 1.0, 0.0)


def _sample_select(q, k_mean, n_blocks):
    n, h, t8, hd = q.shape
    spec = lambda rows: pl.BlockSpec((None, h, rows, hd), lambda i: (i, 0, 0, 0))
    return pl.pallas_call(
        functools.partial(_sample_select_kernel, n_blocks=n_blocks),
        grid=(n,),
        in_specs=[spec(t8), spec(V7X_LANES)],
        out_specs=pl.BlockSpec((None, h, t8, V7X_LANES), lambda i: (i, 0, 0, 0)),
        out_shape=jax.ShapeDtypeStruct((n, h, t8, V7X_LANES), F32),
        compiler_params=_params(1),
        name="sample_select",
    )(q, k_mean)


def _sample_attn_kernel(pt_ref, q_ref, kn_ref, vn_ref, sel_ref, ka_ref, kb_ref, va_ref, vb_ref,
                        o_ref, m_ref, l_ref, acc_ref, *, t_new):
    del pt_ref
    j = pl.program_id(1)
    q = q_ref[...]
    q_bf = q.astype(BF16)
    t_id = lax.broadcasted_iota(jnp.int32, (q.shape[0], q.shape[1], 1), 1)

    @pl.when(j == 0)
    def _():
        qf = q_bf.astype(F32)
        kf = kn_ref[...].astype(BF16).astype(F32)
        vf = vn_ref[...].astype(BF16).astype(F32)
        s = [jnp.sum(qf * kf[:, c:c + 1, :], axis=-1, keepdims=True) * ATTN_SCALE for c in range(t_new)]
        s = [jnp.where(c <= t_id, s[c], NEG_INF) for c in range(t_new)]
        m0 = s[0]
        for c in range(1, t_new):
            m0 = jnp.maximum(m0, s[c])
        l0 = jnp.zeros_like(m0)
        acc = jnp.zeros(acc_ref.shape, F32)
        for c in range(t_new):
            p = jnp.exp(s[c] - m0)
            l0 = l0 + p
            acc = acc + p.astype(BF16).astype(F32) * vf[:, c:c + 1, :]
        m_ref[...] = m0
        l_ref[...] = l0
        acc_ref[...] = acc

    lane = lax.broadcasted_iota(jnp.int32, sel_ref.shape, 2)
    picked = jnp.max(jnp.where(lane == j, sel_ref[...], 0.0), axis=-1, keepdims=True) > 0.0
    qk = functools.partial(lax.dot_general, dimension_numbers=_BATCH_CONTRACT_LAST, preferred_element_type=F32)
    pv = functools.partial(lax.dot_general, dimension_numbers=_BATCH_MATMUL, preferred_element_type=F32)
    sa = jnp.where(picked, qk(q_bf, ka_ref[...].astype(BF16)) * ATTN_SCALE, NEG_INF)
    sb = jnp.where(picked, qk(q_bf, kb_ref[...].astype(BF16)) * ATTN_SCALE, NEG_INF)
    m_old = m_ref[...]
    m_new = jnp.maximum(m_old, jnp.maximum(jnp.max(sa, axis=-1, keepdims=True), jnp.max(sb, axis=-1, keepdims=True)))
    alpha = jnp.exp(m_old - m_new)
    pa = jnp.exp(sa - m_new)
    pb = jnp.exp(sb - m_new)
    l_ref[...] = alpha * l_ref[...] + jnp.sum(pa, axis=-1, keepdims=True) + jnp.sum(pb, axis=-1, keepdims=True)
    acc_ref[...] = (alpha * acc_ref[...] + pv(pa.astype(BF16), va_ref[...].astype(BF16))
                    + pv(pb.astype(BF16), vb_ref[...].astype(BF16)))
    m_ref[...] = m_new

    @pl.when(j == pl.num_programs(1) - 1)
    def _():
        o_ref[...] = acc_ref[...] / l_ref[...]


def _sample_attn(q, k_new, v_new, sel, cache_k, cache_v, page_table, layer, n_blocks, t_new):
    n, h, t8, hd = q.shape
    tok = pl.BlockSpec((None, h, t8, hd), lambda i, j, pt: (i, 0, 0, 0))
    page = lambda which: pl.BlockSpec(
        (None, None, h, PAGE_SIZE, hd),
        lambda i, j, pt: (pt[i, PAGES_PER_BLOCK * j + which], layer, 0, 0, 0))
    return pl.pallas_call(
        functools.partial(_sample_attn_kernel, t_new=t_new),
        grid_spec=pltpu.PrefetchScalarGridSpec(
            num_scalar_prefetch=1, grid=(n, n_blocks),
            in_specs=[tok, tok, tok, pl.BlockSpec((None, h, t8, V7X_LANES), lambda i, j, pt: (i, 0, 0, 0)),
                      page(0), page(1), page(0), page(1)],
            out_specs=tok,
            scratch_shapes=[pltpu.VMEM((h, t8, 1), F32), pltpu.VMEM((h, t8, 1), F32), pltpu.VMEM((h, t8, hd), F32)]),
        out_shape=jax.ShapeDtypeStruct((n, h, t8, hd), F32),
        compiler_params=_params(2),
        name="sample_attn",
    )(page_table, q, k_new, v_new, sel, cache_k, cache_k, cache_v, cache_v)


def _top_values(s, k):
    vals = []
    for r in range(k):
        m = jnp.max(s, axis=0, keepdims=True)
        vals.append(m)
        if r + 1 < k:
            s = jnp.where(s == m, NEG_INF, s)
    return vals


_PEER_CANDIDATES = [(a, b) for a in range(PEER_TOPK) for b in range(PEER_TOPK // (a + 1))]
_PEER_CAND_ROWS = -(-len(_PEER_CANDIDATES) // V7X_SUBLANES) * V7X_SUBLANES


def _peer_route_kernel(x_ref, wq_ref, sk_ref, s2_ref, e2_ref, tau_ref, e1_ref, cand_ref):
    q = _dot(x_ref[...].astype(BF16), wq_ref[...]).astype(BF16)
    cand_ref[...] = jnp.full(cand_ref.shape, NEG_INF, F32)
    for h in range(PEER_HEADS):
        s = []
        for p in range(2):
            c0 = (2 * h + p) * PEER_HALF
            s.append(_dot_nt(sk_ref[h, p], q[:, c0:c0 + PEER_HALF]))
        v1 = _top_values(s[0], PEER_TOPK)
        v2 = _top_values(s[1], PEER_TOPK)
        for r, (a, b) in enumerate(_PEER_CANDIDATES):
            cand_ref[r:r + 1, :] = v1[a] + v2[b]
        cand = cand_ref[...]
        thr = _top_values(cand, PEER_TOPK)[-1]
        z = jnp.sum(jnp.where(cand >= thr, jnp.exp(cand - (v1[0] + v2[0])), 0.0), axis=0, keepdims=True)
        s2_ref[h] = s[1]
        e2_ref[h] = jnp.exp(s[1] - v2[0])
        tau_ref[h] = thr - s[0]
        e1_ref[h] = jnp.exp(s[0] - v1[0]) / z


def _peer_route(x, w_q, sub_keys, tm):
    m, d = x.shape
    out_spec = pl.BlockSpec((PEER_HEADS, PEER_NKEYS, tm), lambda i: (0, 0, i))
    out_shape = jax.ShapeDtypeStruct((PEER_HEADS, PEER_NKEYS, m), F32)
    return pl.pallas_call(
        _peer_route_kernel,
        grid=(m // tm,),
        in_specs=[pl.BlockSpec((tm, d), lambda i: (i, 0)),
                  pl.BlockSpec(w_q.shape, lambda i: (0, 0)),
                  pl.BlockSpec(sub_keys.shape, lambda i: (0, 0, 0, 0))],
        out_specs=[out_spec] * 4,
        out_shape=[out_shape] * 4,
        scratch_shapes=[pltpu.VMEM((_PEER_CAND_ROWS, tm), F32)],
        compiler_params=_params(1),
        name="peer_route",
    )(x, w_q, sub_keys)


def _gelu(x):
    return 0.5 * x * (1.0 + lax.erf(x * (2.0 ** -0.5)))


def _peer_expert_kernel(x_ref, u_ref, v_ref, s2_ref, e2_ref, tau_ref, e1_ref, g_ref, b_ref,
                        o_ref, acc_ref, st_ref, aw_ref, *, te, tm):
    c = pl.program_id(1)

    @pl.when(c == 0)
    def _():
        acc_ref[...] = jnp.zeros(acc_ref.shape, F32)

    st_ref[...] = _dot_nt(u_ref[...], x_ref[...].astype(BF16))
    rows_per_chunk = te // PEER_NKEYS
    chunks_per_group = V7X_SUBLANES // rows_per_chunk
    group = pl.multiple_of((c // chunks_per_group) * V7X_SUBLANES, V7X_SUBLANES)
    sub = c % chunks_per_group

    def row_of(x8, il):
        r = x8[il:il + 1, :]
        for s in range(1, chunks_per_group):
            o = s * rows_per_chunk + il
            r = jnp.where(sub == s, x8[o:o + 1, :], r)
        return r

    for il in range(rows_per_chunk):
        rows = slice(il * PEER_NKEYS, (il + 1) * PEER_NKEYS)
        for tb in range(tm // V7X_LANES):
            cols = slice(tb * V7X_LANES, (tb + 1) * V7X_LANES)
            w = jnp.zeros((PEER_NKEYS, V7X_LANES), F32)
            for h in range(PEER_HEADS):
                tau = row_of(tau_ref[h, pl.ds(group, V7X_SUBLANES), cols], il)
                e1 = row_of(e1_ref[h, pl.ds(group, V7X_SUBLANES), cols], il)
                w = w + jnp.where(s2_ref[h, :, cols] >= tau, e2_ref[h, :, cols] * e1, 0.0)
            aw_ref[rows, cols] = (_gelu(st_ref[rows, cols]) * w).astype(BF16)
    acc_ref[...] += lax.dot_general(aw_ref[...], v_ref[...], _CONTRACT_FIRST, preferred_element_type=F32)

    @pl.when(c == pl.num_programs(1) - 1)
    def _():
        o_ref[...] = _layer_norm(DEEPNORM_ALPHA * x_ref[...] + acc_ref[...], g_ref[...], b_ref[...])


def _peer_experts(x, u, v, route, g, b, tm, te):
    m, d = x.shape
    n_exp = u.shape[0]
    tok = pl.BlockSpec((tm, d), lambda i, c: (i, 0))
    tab = pl.BlockSpec((te, d), lambda i, c: (c, 0))
    rt = pl.BlockSpec((PEER_HEADS, PEER_NKEYS, tm), lambda i, c: (0, 0, i))
    vec = pl.BlockSpec((1, d), lambda i, c: (0, 0))
    return pl.pallas_call(
        functools.partial(_peer_expert_kernel, te=te, tm=tm),
        grid=(m // tm, n_exp // te),
        in_specs=[tok, tab, tab, rt, rt, rt, rt, vec, vec],
        out_specs=tok,
        out_shape=jax.ShapeDtypeStruct((m, d), F32),
        scratch_shapes=[pltpu.VMEM((tm, d), F32), pltpu.VMEM((te, tm), F32), pltpu.VMEM((te, tm), BF16)],
        compiler_params=_params(2),
        name="peer_experts",
    )(x, u, v, *route, g, b)


def _peer_layer(x, w_q, sub_keys, u, v, g, b, tm, te):
    route = _peer_route(x, w_q, sub_keys, tm)
    return _peer_experts(x, u, v, route, g, b, tm, te)


def _row_tile(m, pref):
    return min(m, pref)


def kernel(x_prompt, x_sample, cache_k, cache_v, state_conv, page_table, attn_w_qkv, attn_w_o,
           conv_w_in, conv_w, conv_w_out, ln_mix_g, ln_mix_b, ln_ffn_g, ln_ffn_b,
           peer_w_q, peer_sub_keys, peer_u, peer_v):
    batch, seq, d = x_prompt.shape
    n_dec, t_new, _ = x_sample.shape
    n_past_pages = page_table.shape[1]
    n_past_blocks = n_past_pages // PAGES_PER_BLOCK
    t8 = V7X_SUBLANES

    hp = x_prompt.reshape(batch * seq, d)
    hs = x_sample.reshape(n_dec * t_new, d)
    vec = lambda a: a.reshape(1, d)
    bf = lambda a: a.astype(BF16)

    def peer(h, layer):
        m = h.shape[0]
        return _peer_layer(h, bf(peer_w_q[layer]), bf(peer_sub_keys[layer]), bf(peer_u[layer]), bf(peer_v[layer]),
                           vec(ln_ffn_g[layer]), vec(ln_ffn_b[layer]), tm=_row_tile(m, 512), te=512)

    w_qkv = bf(attn_w_qkv[0])
    w_o = bf(attn_w_o[0])
    qkv_p = _mm(hp, w_qkv, tm=512, tn=1024)
    qkv_s = _mm(hs, w_qkv, tm=hs.shape[0], tn=1024)
    o_p, new_k_prompt, new_v_prompt = _moba_prompt(qkv_p, batch, seq)

    heads = lambda a: a.reshape(n_dec, t_new, N_HEADS, HEAD_DIM).transpose(0, 2, 1, 3)
    q_s, k_s, v_s = [heads(a) for a in jnp.split(qkv_s, 3, axis=-1)]
    pad_t = lambda a: jnp.pad(a, ((0, 0), (0, 0), (0, t8 - t_new), (0, 0)))
    k_mean = _block_key_means(cache_k, page_table, 0, n_past_blocks)
    k_mean = jnp.pad(k_mean.transpose(0, 2, 1, 3), ((0, 0), (0, 0), (0, V7X_LANES - n_past_blocks), (0, 0)))
    sel = _sample_select(pad_t(q_s), k_mean, n_past_blocks)
    o_s = _sample_attn(pad_t(q_s), pad_t(k_s), pad_t(v_s), sel, cache_k, cache_v, page_table, 0,
                       n_past_blocks, t_new)
    o_s = o_s[:, :, :t_new].transpose(0, 2, 1, 3).reshape(n_dec * t_new, d)

    hp = _mm_res_ln(o_p, w_o, hp, vec(ln_mix_g[0]), vec(ln_mix_b[0]), tm=256)
    hs = _mm_res_ln(o_s, w_o, hs, vec(ln_mix_g[0]), vec(ln_mix_b[0]), tm=hs.shape[0])
    hp = peer(hp, 0)
    hs = peer(hs, 0)

    w_in = bf(conv_w_in[0])
    w_out = bf(conv_w_out[0])
    bg_p, z_p = _conv_in(hp, w_in, tm=512, tn=512)
    bg_s, z_s = _conv_in(hs, w_in, tm=hs.shape[0], tn=512)
    hp = _conv_out_prompt(bg_p, z_p, conv_w[0], w_out, hp, vec(ln_mix_g[1]), vec(ln_mix_b[1]), tm=256, seq_len=seq)
    time_major = lambda a: a.reshape(n_dec, t_new, d).transpose(1, 0, 2)
    u_s, conv_state_s = _conv_sample(time_major(bg_s), time_major(z_s), state_conv[0].transpose(1, 0, 2), conv_w[0])
    hs = _mm_res_ln(u_s.transpose(1, 0, 2).reshape(n_dec * t_new, d), w_out, hs,
                    vec(ln_mix_g[1]), vec(ln_mix_b[1]), tm=hs.shape[0])
    hp = peer(hp, 1)
    hs = peer(hs, 1)

    new_k_sample = k_s[:, None]
    new_v_sample = v_s[:, None]
    new_conv_prompt = z_p.reshape(batch, seq, d)[:, seq - (CONV_WIDTH - 1):][None]
    new_conv_sample = conv_state_s.transpose(1, 0, 2)[None]
    return (hp.reshape(batch, seq, d), hs.reshape(n_dec, t_new, d), new_k_prompt, new_v_prompt,
            new_k_sample, new_v_sample, new_conv_prompt, new_conv_sample)
```

```python
import functools

import jax
import jax.numpy as jnp
from jax import lax
from jax.experimental import pallas as pl
from jax.experimental.pallas import tpu as pltpu

F32 = jnp.float32
BF16 = jnp.bfloat16

D_MODEL = 2048
N_HEADS = 16
HEAD_DIM = 128
PAGE_SIZE = 128
MOBA_BLOCK = 256
MOBA_TOPK = 3
PAGES_PER_BLOCK = MOBA_BLOCK // PAGE_SIZE
CONV_WIDTH = 3
PEER_HEADS = 8
PEER_NKEYS = 128
PEER_TOPK = 16
PEER_HALF = 128
LN_EPS = 1e-5
DEPTH = 2
DEEPNORM_ALPHA = (2.0 * DEPTH) ** 0.25
ATTN_SCALE = HEAD_DIM ** -0.5
NEG_INF = float("-inf")

V7X_LANES = 128
V7X_SUBLANES = 8
V7X_VMEM_LIMIT_BYTES = 60 * 1024 * 1024

_CONTRACT_LAST = (((1,), (1,)), ((), ()))
_CONTRACT_FIRST = (((0,), (0,)), ((), ()))
_BATCH_CONTRACT_LAST = (((2,), (2,)), ((0,), (0,)))
_BATCH_MATMUL = (((2,), (1,)), ((0,), (0,)))


def _params(n_axes):
    return pltpu.CompilerParams(dimension_semantics=("arbitrary",) * n_axes,
                                vmem_limit_bytes=V7X_VMEM_LIMIT_BYTES)


def _dot(a, b):
    return jnp.dot(a, b, preferred_element_type=F32)


def _dot_nt(a, b):
    return lax.dot_general(a, b, _CONTRACT_LAST, preferred_element_type=F32)


def _split_bf16(x):
    hi = x.astype(BF16)
    lo = (x - hi.astype(F32)).astype(BF16)
    return hi, lo


def _layer_norm(x, g, b):
    mu = jnp.mean(x, axis=-1, keepdims=True)
    xc = x - mu
    var = jnp.mean(xc * xc, axis=-1, keepdims=True)
    return xc * lax.rsqrt(var + LN_EPS) * g + b


def _mm_kernel(x_ref, w_ref, o_ref):
    o_ref[...] = _dot(x_ref[...].astype(BF16), w_ref[...])


def _mm(x, w, tm, tn):
    m, k = x.shape
    n = w.shape[1]
    return pl.pallas_call(
        _mm_kernel,
        grid=(n // tn, m // tm),
        in_specs=[pl.BlockSpec((tm, k), lambda j, i: (i, 0)),
                  pl.BlockSpec((k, tn), lambda j, i: (0, j))],
        out_specs=pl.BlockSpec((tm, tn), lambda j, i: (i, j)),
        out_shape=jax.ShapeDtypeStruct((m, n), F32),
        compiler_params=_params(2),
        name="proj",
    )(x, w)


def _mm_res_ln_kernel(x_ref, w_ref, h_ref, g_ref, b_ref, o_ref):
    y = _dot(x_ref[...].astype(BF16), w_ref[...])
    o_ref[...] = _layer_norm(DEEPNORM_ALPHA * h_ref[...] + y, g_ref[...], b_ref[...])


def _mm_res_ln(x, w, h, g, b, tm):
    m, k = x.shape
    d = w.shape[1]
    row = lambda i: (i, 0)
    fixed = lambda i: (0, 0)
    return pl.pallas_call(
        _mm_res_ln_kernel,
        grid=(m // tm,),
        in_specs=[pl.BlockSpec((tm, k), row), pl.BlockSpec((k, d), fixed),
                  pl.BlockSpec((tm, d), row), pl.BlockSpec((1, d), fixed), pl.BlockSpec((1, d), fixed)],
        out_specs=pl.BlockSpec((tm, d), row),
        out_shape=jax.ShapeDtypeStruct((m, d), F32),
        compiler_params=_params(1),
        name="out_proj_ln",
    )(x, w, h, g, b)


def _conv_in_kernel(x_ref, wb_ref, wc_ref, wh_ref, bg_ref, z_ref):
    xb = x_ref[...].astype(BF16)
    bg_ref[...] = _dot(xb, wb_ref[...])
    z_ref[...] = _dot(xb, wc_ref[...]) * _dot(xb, wh_ref[...])


def _conv_in(x, w_in, tm, tn):
    m, k = x.shape
    d = w_in.shape[1] // 3
    nb = d // tn
    xs = pl.BlockSpec((tm, k), lambda j, i: (i, 0))
    ws = [pl.BlockSpec((k, tn), functools.partial(lambda j, i, off: (0, j + off), off=part * nb))
          for part in range(3)]
    os_ = pl.BlockSpec((tm, tn), lambda j, i: (i, j))
    return pl.pallas_call(
        _conv_in_kernel,
        grid=(nb, m // tm),
        in_specs=[xs] + ws,
        out_specs=[os_, os_],
        out_shape=[jax.ShapeDtypeStruct((m, d), F32)] * 2,
        compiler_params=_params(2),
        name="conv_in",
    )(x, w_in, w_in, w_in)


def _conv_out_prompt_kernel(bg_ref, z_ref, zp_ref, cw_ref, w_ref, h_ref, g_ref, b_ref, o_ref, *, tiles_per_seq):
    i = pl.program_id(0)
    z = z_ref[...]
    seq_start = (i % tiles_per_seq) == 0
    zp = jnp.where(seq_start, 0.0, zp_ref[...])
    row = lax.broadcasted_iota(jnp.int32, z.shape, 0)
    last = V7X_SUBLANES - 1
    z1 = jnp.where(row == 0, zp[last:last + 1, :], pltpu.roll(z, 1, 0))
    z2 = jnp.where(row == 0, zp[last - 1:last, :],
                   jnp.where(row == 1, zp[last:last + 1, :], pltpu.roll(z, 2, 0)))
    cw = cw_ref[...]
    y = cw[0:1, :] * z2 + cw[1:2, :] * z1 + cw[2:3, :] * z
    u = (bg_ref[...] * y).astype(BF16)
    o_ref[...] = _layer_norm(DEEPNORM_ALPHA * h_ref[...] + _dot(u, w_ref[...]), g_ref[...], b_ref[...])


def _conv_out_prompt(bg, z, conv_w, w_out, h, g, b, tm, seq_len):
    m, d = z.shape
    row = lambda i: (i, 0)
    fixed = lambda i: (0, 0)
    halo = lambda i: (jnp.maximum(i * (tm // V7X_SUBLANES) - 1, 0), 0)
    return pl.pallas_call(
        functools.partial(_conv_out_prompt_kernel, tiles_per_seq=seq_len // tm),
        grid=(m // tm,),
        in_specs=[pl.BlockSpec((tm, d), row), pl.BlockSpec((tm, d), row),
                  pl.BlockSpec((V7X_SUBLANES, d), halo), pl.BlockSpec((CONV_WIDTH, d), fixed),
                  pl.BlockSpec((d, d), fixed), pl.BlockSpec((tm, d), row),
                  pl.BlockSpec((1, d), fixed), pl.BlockSpec((1, d), fixed)],
        out_specs=pl.BlockSpec((tm, d), row),
        out_shape=jax.ShapeDtypeStruct((m, d), F32),
        compiler_params=_params(1),
        name="conv_out_prompt",
    )(bg, z, z, conv_w, w_out, h, g, b)


def _conv_sample_kernel(bg_ref, z_ref, st_ref, cw_ref, u_ref, ns_ref, *, t):
    cw = cw_ref[...]
    zp = [st_ref[j] for j in range(CONV_WIDTH - 1)] + [z_ref[j] for j in range(t)]
    for j in range(t):
        y = cw[0:1, :] * zp[j] + cw[1:2, :] * zp[j + 1] + cw[2:3, :] * zp[j + 2]
        u_ref[j] = bg_ref[j] * y
    for j in range(CONV_WIDTH - 1):
        ns_ref[j] = zp[t + j]


def _conv_sample(bg, z, state, conv_w):
    t, n, d = z.shape
    return pl.pallas_call(
        functools.partial(_conv_sample_kernel, t=t),
        out_shape=[jax.ShapeDtypeStruct((t, n, d), F32), jax.ShapeDtypeStruct((CONV_WIDTH - 1, n, d), F32)],
        name="conv_sample",
    )(bg, z, state, conv_w)


def _topk_mask(g, n_valid, lane, n_cand):
    rank = jnp.zeros(g.shape, F32)
    for c in range(n_cand):
        col = g[..., c:c + 1]
        beats = (col > g) | ((col == g) & (c < lane))
        rank = rank + jnp.where(beats & (c < n_valid), 1.0, 0.0)
    return (lane < n_valid) & (rank < MOBA_TOPK)


def _moba_prompt_kernel(q_ref, k_ref, v_ref, o_ref, kp_ref, vp_ref, km_ref, *, n_blk):
    blk = MOBA_BLOCK
    k = k_ref[...]
    v = v_ref[...]
    kp_ref[...] = k.reshape(kp_ref.shape)
    vp_ref[...] = v.reshape(vp_ref.shape)
    k_bf = k.astype(BF16)
    vt_bf = v.T.astype(BF16)
    km_ref[...] = jnp.zeros(km_ref.shape, F32)
    for n in range(n_blk):
        km_ref[n:n + 1, :] = jnp.mean(k[n * blk:(n + 1) * blk, :], axis=0, keepdims=True)
    q_hi, q_lo = _split_bf16(q_ref[...])
    km_hi, km_lo = _split_bf16(km_ref[...])
    gate_t = _dot_nt(km_hi, q_hi) + _dot_nt(km_hi, q_lo) + _dot_nt(km_lo, q_hi)
    blk_row = lax.broadcasted_iota(jnp.int32, (km_ref.shape[0], blk), 0)
    key_id = lax.broadcasted_iota(jnp.int32, (blk, blk), 0)
    qry_id = lax.broadcasted_iota(jnp.int32, (blk, blk), 1)

    for qb in range(n_blk):
        qs = slice(qb * blk, (qb + 1) * blk)
        n_keys = (qb + 1) * blk
        s = _dot_nt(k_bf[:n_keys], q_hi[qs]) * ATTN_SCALE
        if qb > MOBA_TOPK:
            g = gate_t[:, qs]
            rank = jnp.zeros(g.shape, F32)
            for c in range(qb):
                gc = g[c:c + 1, :]
                beats = (gc > g) | ((gc == g) & (c < blk_row))
                rank = rank + jnp.where(beats, 1.0, 0.0)
            sel = jnp.where((blk_row < qb) & (rank < MOBA_TOPK), 1.0, 0.0)
        pieces = []
        for n in range(qb + 1):
            sn = s[n * blk:(n + 1) * blk]
            if n == qb:
                sn = jnp.where(key_id <= qry_id, sn, NEG_INF)
            elif qb > MOBA_TOPK:
                sn = jnp.where(sel[n:n + 1, :] > 0.5, sn, NEG_INF)
            pieces.append(sn)
        m = jnp.max(pieces[0], axis=0, keepdims=True)
        for sn in pieces[1:]:
            m = jnp.maximum(m, jnp.max(sn, axis=0, keepdims=True))
        p = [jnp.exp(sn - m) for sn in pieces]
        l = jnp.sum(p[0], axis=0, keepdims=True)
        for pn in p[1:]:
            l = l + jnp.sum(pn, axis=0, keepdims=True)
        p_bf = jnp.concatenate([pn.astype(BF16) for pn in p], axis=0)
        out_t = _dot(vt_bf[:, :n_keys], p_bf) * (1.0 / l)
        o_ref[qs, :] = out_t.T.astype(o_ref.dtype)


def _moba_prompt(qkv, batch, seq):
    n_blk = seq // MOBA_BLOCK
    n_pages = seq // PAGE_SIZE
    hd = HEAD_DIM
    km_rows = -(-n_blk // V7X_SUBLANES) * V7X_SUBLANES
    col = lambda part: pl.BlockSpec((seq, hd), lambda b, h: (b, part * N_HEADS + h))
    page_spec = pl.BlockSpec((None, n_pages, None, None, PAGE_SIZE, hd), lambda b, h: (b, 0, 0, h, 0, 0))
    page_shape = jax.ShapeDtypeStruct((batch, n_pages, 1, N_HEADS, PAGE_SIZE, hd), F32)
    return pl.pallas_call(
        functools.partial(_moba_prompt_kernel, n_blk=n_blk),
        grid=(batch, N_HEADS),
        in_specs=[col(0), col(1), col(2)],
        out_specs=[pl.BlockSpec((seq, hd), lambda b, h: (b, h)), page_spec, page_spec],
        out_shape=[jax.ShapeDtypeStruct((batch * seq, D_MODEL), BF16), page_shape, page_shape],
        scratch_shapes=[pltpu.VMEM((km_rows, hd), F32)],
        compiler_params=_params(2),
        name="moba_prompt",
    )(qkv, qkv, qkv)


_BATCH_QK = functools.partial(lax.dot_general, dimension_numbers=_BATCH_CONTRACT_LAST, preferred_element_type=F32)
_BATCH_PV = functools.partial(lax.dot_general, dimension_numbers=_BATCH_MATMUL, preferred_element_type=F32)

_DEC_PAGES_PER_STEP = 2 * PAGES_PER_BLOCK


def _page_specs(n_heads, layer):
    def spec(which):
        return pl.BlockSpec((None, None, n_heads, PAGE_SIZE, HEAD_DIM),
                            lambda i, j, pt: (pt[i, _DEC_PAGES_PER_STEP * j + which], layer, 0, 0, 0))
    return [spec(w) for w in range(_DEC_PAGES_PER_STEP)]


def _dec_k_pass_kernel(pt_ref, q_ref, *refs):
    del pt_ref
    k_refs = refs[:_DEC_PAGES_PER_STEP]
    km_ref, s_ref = refs[_DEC_PAGES_PER_STEP:]
    q_bf = q_ref[...].astype(BF16)
    sums = []
    for pg, k_ref in enumerate(k_refs):
        k = k_ref[...]
        sums.append(jnp.sum(k, axis=1))
        s_ref[:, :, pg * PAGE_SIZE:(pg + 1) * PAGE_SIZE] = _BATCH_QK(q_bf, k.astype(BF16)) * ATTN_SCALE
    for b in range(_DEC_PAGES_PER_STEP // PAGES_PER_BLOCK):
        tot = sums[b * PAGES_PER_BLOCK]
        for pg in range(1, PAGES_PER_BLOCK):
            tot = tot + sums[b * PAGES_PER_BLOCK + pg]
        km_ref[b] = tot / MOBA_BLOCK


def _dec_k_pass(q, cache_k, page_table, layer):
    n, h, t8, hd = q.shape
    n_pages = page_table.shape[1]
    n_blocks = n_pages // PAGES_PER_BLOCK
    blocks_per_step = _DEC_PAGES_PER_STEP // PAGES_PER_BLOCK
    keys_per_step = _DEC_PAGES_PER_STEP * PAGE_SIZE
    return pl.pallas_call(
        _dec_k_pass_kernel,
        grid_spec=pltpu.PrefetchScalarGridSpec(
            num_scalar_prefetch=1, grid=(n, n_pages // _DEC_PAGES_PER_STEP),
            in_specs=[pl.BlockSpec((None, h, t8, hd), lambda i, j, pt: (i, 0, 0, 0))] + _page_specs(h, layer),
            out_specs=[pl.BlockSpec((None, blocks_per_step, h, hd), lambda i, j, pt: (i, j, 0, 0)),
                       pl.BlockSpec((None, h, t8, keys_per_step), lambda i, j, pt: (i, 0, 0, j))]),
        out_shape=[jax.ShapeDtypeStruct((n, n_blocks, h, hd), F32),
                   jax.ShapeDtypeStruct((n, h, t8, n_pages * PAGE_SIZE), F32)],
        compiler_params=_params(2),
        name="dec_k_pass",
    )(page_table, q, *([cache_k] * _DEC_PAGES_PER_STEP))


def _dec_softmax_kernel(q_ref, km_ref, s_ref, kn_ref, p_ref, pown_ref, *, n_blocks, t_new):
    q_hi, q_lo = _split_bf16(q_ref[...])
    km_hi, km_lo = _split_bf16(km_ref[...])
    gate = _BATCH_QK(q_hi, km_hi) + _BATCH_QK(q_lo, km_hi) + _BATCH_QK(q_hi, km_lo)
    lane = lax.broadcasted_iota(jnp.int32, gate.shape, 2)
    sel = jnp.where(_topk_mask(gate, n_blocks, lane, n_blocks), 1.0, 0.0)

    t_id = lax.broadcasted_iota(jnp.int32, (gate.shape[0], gate.shape[1], 1), 1)
    qf = q_hi.astype(F32)
    kf = kn_ref[...].astype(BF16).astype(F32)
    s_own = [jnp.where(c <= t_id, jnp.sum(qf * kf[:, c:c + 1, :], axis=-1, keepdims=True) * ATTN_SCALE, NEG_INF)
             for c in range(t_new)]
    m = s_own[0]
    for c in range(1, t_new):
        m = jnp.maximum(m, s_own[c])
    pieces = []
    for b in range(n_blocks):
        sb = jnp.where(sel[:, :, b:b + 1] > 0.5, s_ref[:, :, b * MOBA_BLOCK:(b + 1) * MOBA_BLOCK], NEG_INF)
        pieces.append(sb)
        m = jnp.maximum(m, jnp.max(sb, axis=-1, keepdims=True))
    e_own = [jnp.exp(s - m) for s in s_own]
    l = e_own[0]
    for e in e_own[1:]:
        l = l + e
    e_past = [jnp.exp(sb - m) for sb in pieces]
    for e in e_past:
        l = l + jnp.sum(e, axis=-1, keepdims=True)
    for b, e in enumerate(e_past):
        p_ref[:, :, b * MOBA_BLOCK:(b + 1) * MOBA_BLOCK] = e / l
    own = jnp.zeros(pown_ref.shape, F32)
    own_lane = lax.broadcasted_iota(jnp.int32, pown_ref.shape, 2)
    for c in range(t_new):
        own = jnp.where(own_lane == c, e_own[c] / l, own)
    pown_ref[...] = own


def _dec_softmax(q, k_mean, scores, k_new, n_blocks, t_new):
    n, h, t8, hd = q.shape
    past = scores.shape[-1]
    spec = lambda rows, cols: pl.BlockSpec((None, h, rows, cols), lambda i: (i, 0, 0, 0))
    return pl.pallas_call(
        functools.partial(_dec_softmax_kernel, n_blocks=n_blocks, t_new=t_new),
        grid=(n,),
        in_specs=[spec(t8, hd), spec(V7X_LANES, hd), spec(t8, past), spec(t8, hd)],
        out_specs=[spec(t8, past), spec(t8, V7X_LANES)],
        out_shape=[jax.ShapeDtypeStruct((n, h, t8, past), F32), jax.ShapeDtypeStruct((n, h, t8, V7X_LANES), F32)],
        compiler_params=_params(1),
        name="dec_softmax",
    )(q, k_mean, scores, k_new)


def _dec_v_pass_kernel(pt_ref, p_ref, pown_ref, vn_ref, *refs, t_new):
    del pt_ref
    v_refs = refs[:_DEC_PAGES_PER_STEP]
    o_ref = refs[_DEC_PAGES_PER_STEP]

    @pl.when(pl.program_id(1) == 0)
    def _():
        p_own = pown_ref[...].astype(BF16).astype(F32)
        vf = vn_ref[...].astype(BF16).astype(F32)
        acc = jnp.zeros(o_ref.shape, F32)
        for c in range(t_new):
            acc = acc + p_own[:, :, c:c + 1] * vf[:, c:c + 1, :]
        o_ref[...] = acc

    tot = o_ref[...]
    for pg, v_ref in enumerate(v_refs):
        p = p_ref[:, :, pg * PAGE_SIZE:(pg + 1) * PAGE_SIZE].astype(BF16)
        tot = tot + _BATCH_PV(p, v_ref[...].astype(BF16))
    o_ref[...] = tot


def _dec_v_pass(probs, p_own, v_new, cache_v, page_table, layer, t_new):
    n, h, t8, hd = v_new.shape
    n_pages = page_table.shape[1]
    keys_per_step = _DEC_PAGES_PER_STEP * PAGE_SIZE
    tok = lambda cols: pl.BlockSpec((None, h, t8, cols), lambda i, j, pt: (i, 0, 0, 0))
    return pl.pallas_call(
        functools.partial(_dec_v_pass_kernel, t_new=t_new),
        grid_spec=pltpu.PrefetchScalarGridSpec(
            num_scalar_prefetch=1, grid=(n, n_pages // _DEC_PAGES_PER_STEP),
            in_specs=[pl.BlockSpec((None, h, t8, keys_per_step), lambda i, j, pt: (i, 0, 0, j)),
                      tok(V7X_LANES), tok(hd)] + _page_specs(h, layer),
            out_specs=tok(hd)),
        out_shape=jax.ShapeDtypeStruct((n, h, t8, hd), F32),
        compiler_params=_params(2),
        name="dec_v_pass",
    )(page_table, probs, p_own, v_new, *([cache_v] * _DEC_PAGES_PER_STEP))


def _top_values(s, k):
    vals = []
    for r in range(k):
        m = jnp.max(s, axis=0, keepdims=True)
        vals.append(m)
        if r + 1 < k:
            s = jnp.where(s == m, NEG_INF, s)
    return vals


_PEER_CANDIDATES = [(a, b) for a in range(PEER_TOPK) for b in range(PEER_TOPK // (a + 1))]
_PEER_CAND_ROWS = -(-len(_PEER_CANDIDATES) // V7X_SUBLANES) * V7X_SUBLANES


def _peer_route_kernel(x_ref, wq_ref, sk_ref, s2_ref, e2_ref, tau_ref, e1_ref, cand_ref):
    q = _dot(x_ref[...].astype(BF16), wq_ref[...]).astype(BF16)
    cand_ref[...] = jnp.full(cand_ref.shape, NEG_INF, F32)
    for h in range(PEER_HEADS):
        s = []
        for p in range(2):
            c0 = (2 * h + p) * PEER_HALF
            s.append(_dot_nt(sk_ref[h, p], q[:, c0:c0 + PEER_HALF]))
        v1 = _top_values(s[0], PEER_TOPK)
        v2 = _top_values(s[1], PEER_TOPK)
        for r, (a, b) in enumerate(_PEER_CANDIDATES):
            cand_ref[r:r + 1, :] = v1[a] + v2[b]
        cand = cand_ref[...]
        thr = _top_values(cand, PEER_TOPK)[-1]
        z = jnp.sum(jnp.where(cand >= thr, jnp.exp(cand - (v1[0] + v2[0])), 0.0), axis=0, keepdims=True)
        s2_ref[h] = s[1]
        e2_ref[h] = jnp.exp(s[1] - v2[0])
        tau_ref[h] = thr - s[0]
        e1_ref[h] = 0.5 * jnp.exp(s[0] - v1[0]) / z


def _peer_route(x, w_q, sub_keys, tm):
    m, d = x.shape
    out_spec = pl.BlockSpec((PEER_HEADS, PEER_NKEYS, tm), lambda i: (0, 0, i))
    out_shape = jax.ShapeDtypeStruct((PEER_HEADS, PEER_NKEYS, m), F32)
    return pl.pallas_call(
        _peer_route_kernel,
        grid=(m // tm,),
        in_specs=[pl.BlockSpec((tm, d), lambda i: (i, 0)),
                  pl.BlockSpec(w_q.shape, lambda i: (0, 0)),
                  pl.BlockSpec(sub_keys.shape, lambda i: (0, 0, 0, 0))],
        out_specs=[out_spec] * 4,
        out_shape=[out_shape] * 4,
        scratch_shapes=[pltpu.VMEM((_PEER_CAND_ROWS, tm), F32)],
        compiler_params=_params(1),
        name="peer_route",
    )(x, w_q, sub_keys)


def _gelu_x2(x):
    return x * (1.0 + lax.erf(x * (2.0 ** -0.5)))


def _peer_expert_kernel(x_ref, u_ref, v_ref, s2_ref, e2_ref, tau_ref, e1_ref, g_ref, b_ref,
                        o_ref, st_ref, aw_ref, *, te, tm):
    c = pl.program_id(1)

    @pl.when(c == 0)
    def _():
        o_ref[...] = jnp.zeros(o_ref.shape, F32)

    st_ref[...] = _dot_nt(u_ref[...], x_ref[...].astype(BF16))

    rows_per_chunk = te // PEER_NKEYS
    chunks_per_group = max(V7X_SUBLANES // rows_per_chunk, 1)
    group = pl.multiple_of((c * rows_per_chunk // V7X_SUBLANES) * V7X_SUBLANES, V7X_SUBLANES)
    part = c % chunks_per_group

    def row_of(ref, h, r, cols):
        base = group + (r // V7X_SUBLANES) * V7X_SUBLANES
        x8 = ref[h, pl.ds(base, V7X_SUBLANES), cols]
        r8 = r % V7X_SUBLANES
        row = x8[r8:r8 + 1, :]
        for s in range(1, chunks_per_group):
            o = s * rows_per_chunk + r8
            row = jnp.where(part == s, x8[o:o + 1, :], row)
        return row

    for r in range(rows_per_chunk):
        rows = slice(r * PEER_NKEYS, (r + 1) * PEER_NKEYS)
        for tb in range(tm // V7X_LANES):
            cols = slice(tb * V7X_LANES, (tb + 1) * V7X_LANES)
            w = jnp.zeros((PEER_NKEYS, V7X_LANES), F32)
            for h in range(PEER_HEADS):
                tau = row_of(tau_ref, h, r, cols)
                e1 = row_of(e1_ref, h, r, cols)
                w = w + jnp.where(s2_ref[h, :, cols] >= tau, e2_ref[h, :, cols] * e1, 0.0)
            aw_ref[rows, cols] = (_gelu_x2(st_ref[rows, cols]) * w).astype(BF16)
    o_ref[...] += lax.dot_general(aw_ref[...], v_ref[...], _CONTRACT_FIRST, preferred_element_type=F32)

    @pl.when(c == pl.num_programs(1) - 1)
    def _():
        o_ref[...] = _layer_norm(DEEPNORM_ALPHA * x_ref[...] + o_ref[...], g_ref[...], b_ref[...])


def _peer_experts(x, u, v, route, g, b, tm, te):
    m, d = x.shape
    n_exp = u.shape[0]
    tok = pl.BlockSpec((tm, d), lambda i, c: (i, 0))
    tab = pl.BlockSpec((te, d), lambda i, c: (c, 0))
    rt = pl.BlockSpec((PEER_HEADS, PEER_NKEYS, tm), lambda i, c: (0, 0, i))
    vec = pl.BlockSpec((1, d), lambda i, c: (0, 0))
    return pl.pallas_call(
        functools.partial(_peer_expert_kernel, te=te, tm=tm),
        grid=(m // tm, n_exp // te),
        in_specs=[tok, tab, tab, rt, rt, rt, rt, vec, vec],
        out_specs=tok,
        out_shape=jax.ShapeDtypeStruct((m, d), F32),
        scratch_shapes=[pltpu.VMEM((te, tm), F32), pltpu.VMEM((te, tm), BF16)],
        compiler_params=_params(2),
        name="peer_experts",
    )(x, u, v, *route, g, b)


def _peer_layer(x, w_q, sub_keys, u, v, g, b, tm, te):
    route = _peer_route(x, w_q, sub_keys, tm)
    return _peer_experts(x, u, v, route, g, b, tm, te)


def _row_tile(m, pref):
    return min(m, pref)


def kernel(x_prompt, x_sample, cache_k, cache_v, state_conv, page_table, attn_w_qkv, attn_w_o,
           conv_w_in, conv_w, conv_w_out, ln_mix_g, ln_mix_b, ln_ffn_g, ln_ffn_b,
           peer_w_q, peer_sub_keys, peer_u, peer_v):
    batch, seq, d = x_prompt.shape
    n_dec, t_new, _ = x_sample.shape
    n_past_pages = page_table.shape[1]
    n_past_blocks = n_past_pages // PAGES_PER_BLOCK
    t8 = V7X_SUBLANES

    hp = x_prompt.reshape(batch * seq, d)
    hs = x_sample.reshape(n_dec * t_new, d)
    vec = lambda a: a.reshape(1, d)
    bf = lambda a: a.astype(BF16)

    def peer(h, layer):
        m = h.shape[0]
        return _peer_layer(h, bf(peer_w_q[layer]), bf(peer_sub_keys[layer]), bf(peer_u[layer]), bf(peer_v[layer]),
                           vec(ln_ffn_g[layer]), vec(ln_ffn_b[layer]), tm=_row_tile(m, 512), te=1024)

    w_qkv = bf(attn_w_qkv[0])
    w_o = bf(attn_w_o[0])
    qkv_p = _mm(hp, w_qkv, tm=512, tn=1024)
    qkv_s = _mm(hs, w_qkv, tm=hs.shape[0], tn=1024)
    o_p, new_k_prompt, new_v_prompt = _moba_prompt(qkv_p, batch, seq)

    heads = lambda a: a.reshape(n_dec, t_new, N_HEADS, HEAD_DIM).transpose(0, 2, 1, 3)
    q_s, k_s, v_s = [heads(a) for a in jnp.split(qkv_s, 3, axis=-1)]
    pad_t = lambda a: jnp.pad(a, ((0, 0), (0, 0), (0, t8 - t_new), (0, 0)))
    assert n_past_pages % _DEC_PAGES_PER_STEP == 0 and n_past_blocks <= V7X_LANES
    k_mean, scores = _dec_k_pass(pad_t(q_s), cache_k, page_table, 0)
    k_mean = jnp.pad(k_mean.transpose(0, 2, 1, 3), ((0, 0), (0, 0), (0, V7X_LANES - n_past_blocks), (0, 0)))
    probs, p_own = _dec_softmax(pad_t(q_s), k_mean, scores, pad_t(k_s), n_past_blocks, t_new)
    o_s = _dec_v_pass(probs, p_own, pad_t(v_s), cache_v, page_table, 0, t_new)
    o_s = o_s[:, :, :t_new].transpose(0, 2, 1, 3).reshape(n_dec * t_new, d)

    hp = _mm_res_ln(o_p, w_o, hp, vec(ln_mix_g[0]), vec(ln_mix_b[0]), tm=256)
    hs = _mm_res_ln(o_s, w_o, hs, vec(ln_mix_g[0]), vec(ln_mix_b[0]), tm=hs.shape[0])
    hp = peer(hp, 0)
    hs = peer(hs, 0)

    w_in = bf(conv_w_in[0])
    w_out = bf(conv_w_out[0])
    bg_p, z_p = _conv_in(hp, w_in, tm=512, tn=512)
    bg_s, z_s = _conv_in(hs, w_in, tm=hs.shape[0], tn=512)
    hp = _conv_out_prompt(bg_p, z_p, conv_w[0], w_out, hp, vec(ln_mix_g[1]), vec(ln_mix_b[1]), tm=256, seq_len=seq)
    time_major = lambda a: a.reshape(n_dec, t_new, d).transpose(1, 0, 2)
    u_s, conv_state_s = _conv_sample(time_major(bg_s), time_major(z_s), state_conv[0].transpose(1, 0, 2), conv_w[0])
    hs = _mm_res_ln(u_s.transpose(1, 0, 2).reshape(n_dec * t_new, d), w_out, hs,
                    vec(ln_mix_g[1]), vec(ln_mix_b[1]), tm=hs.shape[0])
    hp = peer(hp, 1)
    hs = peer(hs, 1)

    new_k_sample = k_s[:, None]
    new_v_sample = v_s[:, None]
    new_conv_prompt = z_p.reshape(batch, seq, d)[:, seq - (CONV_WIDTH - 1):][None]
    new_conv_sample = conv_state_s.transpose(1, 0, 2)[None]
    return (hp.reshape(batch, seq, d), hs.reshape(n_dec, t_new, d), new_k_prompt, new_v_prompt,
            new_k_sample, new_v_sample, new_conv_prompt, new_conv_sample)
```

```python
import functools

import jax
import jax.numpy as jnp
from jax import lax
from jax.experimental import pallas as pl
from jax.experimental.pallas import tpu as pltpu

F32 = jnp.float32
BF16 = jnp.bfloat16

D_MODEL = 2048
N_HEADS = 16
HEAD_DIM = 128
PAGE_SIZE = 128
MOBA_BLOCK = 256
MOBA_TOPK = 3
PAGES_PER_BLOCK = MOBA_BLOCK // PAGE_SIZE
CONV_WIDTH = 3
PEER_HEADS = 8
PEER_NKEYS = 128
PEER_TOPK = 16
PEER_HALF = 128
LN_EPS = 1e-5
DEPTH = 2
DEEPNORM_ALPHA = (2.0 * DEPTH) ** 0.25
ATTN_SCALE = HEAD_DIM ** -0.5
NEG_INF = float("-inf")

V7X_LANES = 128
V7X_SUBLANES = 8
V7X_VMEM_LIMIT_BYTES = 60 * 1024 * 1024

_CONTRACT_LAST = (((1,), (1,)), ((), ()))
_CONTRACT_FIRST = (((0,), (0,)), ((), ()))
_BATCH_CONTRACT_LAST = (((2,), (2,)), ((0,), (0,)))
_BATCH_MATMUL = (((2,), (1,)), ((0,), (0,)))


def _params(n_axes):
    return pltpu.CompilerParams(dimension_semantics=("arbitrary",) * n_axes,
                                vmem_limit_bytes=V7X_VMEM_LIMIT_BYTES)


def _dot(a, b):
    return jnp.dot(a, b, preferred_element_type=F32)


def _dot_nt(a, b):
    return lax.dot_general(a, b, _CONTRACT_LAST, preferred_element_type=F32)


def _split_bf16(x):
    hi = x.astype(BF16)
    lo = (x - hi.astype(F32)).astype(BF16)
    return hi, lo


def _layer_norm(x, g, b):
    mu = jnp.mean(x, axis=-1, keepdims=True)
    xc = x - mu
    var = jnp.mean(xc * xc, axis=-1, keepdims=True)
    return xc * lax.rsqrt(var + LN_EPS) * g + b


def _mm_kernel(x_ref, w_ref, o_ref):
    o_ref[...] = _dot(x_ref[...].astype(BF16), w_ref[...])


def _mm(x, w, tm, tn):
    m, k = x.shape
    n = w.shape[1]
    return pl.pallas_call(
        _mm_kernel,
        grid=(n // tn, m // tm),
        in_specs=[pl.BlockSpec((tm, k), lambda j, i: (i, 0)),
                  pl.BlockSpec((k, tn), lambda j, i: (0, j))],
        out_specs=pl.BlockSpec((tm, tn), lambda j, i: (i, j)),
        out_shape=jax.ShapeDtypeStruct((m, n), F32),
        compiler_params=_params(2),
        name="proj",
    )(x, w)


def _mm_res_ln_kernel(x_ref, w_ref, h_ref, g_ref, b_ref, o_ref):
    y = _dot(x_ref[...].astype(BF16), w_ref[...])
    o_ref[...] = _layer_norm(DEEPNORM_ALPHA * h_ref[...] + y, g_ref[...], b_ref[...])


def _mm_res_ln(x, w, h, g, b, tm):
    m, k = x.shape
    d = w.shape[1]
    row = lambda i: (i, 0)
    fixed = lambda i: (0, 0)
    return pl.pallas_call(
        _mm_res_ln_kernel,
        grid=(m // tm,),
        in_specs=[pl.BlockSpec((tm, k), row), pl.BlockSpec((k, d), fixed),
                  pl.BlockSpec((tm, d), row), pl.BlockSpec((1, d), fixed), pl.BlockSpec((1, d), fixed)],
        out_specs=pl.BlockSpec((tm, d), row),
        out_shape=jax.ShapeDtypeStruct((m, d), F32),
        compiler_params=_params(1),
        name="out_proj_ln",
    )(x, w, h, g, b)


def _conv_in_kernel(x_ref, wb_ref, wc_ref, wh_ref, bg_ref, z_ref):
    xb = x_ref[...].astype(BF16)
    bg_ref[...] = _dot(xb, wb_ref[...])
    z_ref[...] = _dot(xb, wc_ref[...]) * _dot(xb, wh_ref[...])


def _conv_in(x, w_in, tm, tn):
    m, k = x.shape
    d = w_in.shape[1] // 3
    nb = d // tn
    xs = pl.BlockSpec((tm, k), lambda j, i: (i, 0))
    ws = [pl.BlockSpec((k, tn), functools.partial(lambda j, i, off: (0, j + off), off=part * nb))
          for part in range(3)]
    os_ = pl.BlockSpec((tm, tn), lambda j, i: (i, j))
    return pl.pallas_call(
        _conv_in_kernel,
        grid=(nb, m // tm),
        in_specs=[xs] + ws,
        out_specs=[os_, os_],
        out_shape=[jax.ShapeDtypeStruct((m, d), F32)] * 2,
        compiler_params=_params(2),
        name="conv_in",
    )(x, w_in, w_in, w_in)


def _conv_out_prompt_kernel(bg_ref, z_ref, zp_ref, cw_ref, w_ref, h_ref, g_ref, b_ref, o_ref, *, tiles_per_seq):
    i = pl.program_id(0)
    z = z_ref[...]
    seq_start = (i % tiles_per_seq) == 0
    zp = jnp.where(seq_start, 0.0, zp_ref[...])
    row = lax.broadcasted_iota(jnp.int32, z.shape, 0)
    last = V7X_SUBLANES - 1
    z1 = jnp.where(row == 0, zp[last:last + 1, :], pltpu.roll(z, 1, 0))
    z2 = jnp.where(row == 0, zp[last - 1:last, :],
                   jnp.where(row == 1, zp[last:last + 1, :], pltpu.roll(z, 2, 0)))
    cw = cw_ref[...]
    y = cw[0:1, :] * z2 + cw[1:2, :] * z1 + cw[2:3, :] * z
    u = (bg_ref[...] * y).astype(BF16)
    o_ref[...] = _layer_norm(DEEPNORM_ALPHA * h_ref[...] + _dot(u, w_ref[...]), g_ref[...], b_ref[...])


def _conv_out_prompt(bg, z, conv_w, w_out, h, g, b, tm, seq_len):
    m, d = z.shape
    row = lambda i: (i, 0)
    fixed = lambda i: (0, 0)
    halo = lambda i: (jnp.maximum(i * (tm // V7X_SUBLANES) - 1, 0), 0)
    return pl.pallas_call(
        functools.partial(_conv_out_prompt_kernel, tiles_per_seq=seq_len // tm),
        grid=(m // tm,),
        in_specs=[pl.BlockSpec((tm, d), row), pl.BlockSpec((tm, d), row),
                  pl.BlockSpec((V7X_SUBLANES, d), halo), pl.BlockSpec((CONV_WIDTH, d), fixed),
                  pl.BlockSpec((d, d), fixed), pl.BlockSpec((tm, d), row),
                  pl.BlockSpec((1, d), fixed), pl.BlockSpec((1, d), fixed)],
        out_specs=pl.BlockSpec((tm, d), row),
        out_shape=jax.ShapeDtypeStruct((m, d), F32),
        compiler_params=_params(1),
        name="conv_out_prompt",
    )(bg, z, z, conv_w, w_out, h, g, b)


def _conv_sample_kernel(bg_ref, z_ref, st_ref, cw_ref, u_ref, ns_ref, *, t):
    cw = cw_ref[...]
    zp = [st_ref[j] for j in range(CONV_WIDTH - 1)] + [z_ref[j] for j in range(t)]
    for j in range(t):
        y = cw[0:1, :] * zp[j] + cw[1:2, :] * zp[j + 1] + cw[2:3, :] * zp[j + 2]
        u_ref[j] = bg_ref[j] * y
    for j in range(CONV_WIDTH - 1):
        ns_ref[j] = zp[t + j]


def _conv_sample(bg, z, state, conv_w):
    t, n, d = z.shape
    return pl.pallas_call(
        functools.partial(_conv_sample_kernel, t=t),
        out_shape=[jax.ShapeDtypeStruct((t, n, d), F32), jax.ShapeDtypeStruct((CONV_WIDTH - 1, n, d), F32)],
        name="conv_sample",
    )(bg, z, state, conv_w)


def _topk_mask(g, n_valid, lane, n_cand):
    rank = jnp.zeros(g.shape, F32)
    for c in range(n_cand):
        col = g[..., c:c + 1]
        beats = (col > g) | ((col == g) & (c < lane))
        rank = rank + jnp.where(beats & (c < n_valid), 1.0, 0.0)
    return (lane < n_valid) & (rank < MOBA_TOPK)


def _moba_prompt_kernel(q_ref, k_ref, v_ref, o_ref, kp_ref, vp_ref, km_ref, *, n_blk):
    blk = MOBA_BLOCK
    k = k_ref[...]
    v = v_ref[...]
    kp_ref[...] = k.reshape(kp_ref.shape)
    vp_ref[...] = v.reshape(vp_ref.shape)
    k_bf = k.astype(BF16)
    vt_bf = v.T.astype(BF16)
    km_ref[...] = jnp.zeros(km_ref.shape, F32)
    for n in range(n_blk):
        km_ref[n:n + 1, :] = jnp.mean(k[n * blk:(n + 1) * blk, :], axis=0, keepdims=True)
    q_hi, q_lo = _split_bf16(q_ref[...])
    km_hi, km_lo = _split_bf16(km_ref[...])
    gate_t = _dot_nt(km_hi, q_hi) + _dot_nt(km_hi, q_lo) + _dot_nt(km_lo, q_hi)
    blk_row = lax.broadcasted_iota(jnp.int32, (km_ref.shape[0], blk), 0)
    key_id = lax.broadcasted_iota(jnp.int32, (blk, blk), 0)
    qry_id = lax.broadcasted_iota(jnp.int32, (blk, blk), 1)

    for qb in range(n_blk):
        qs = slice(qb * blk, (qb + 1) * blk)
        n_keys = (qb + 1) * blk
        s = _dot_nt(k_bf[:n_keys], q_hi[qs]) * ATTN_SCALE
        if qb > MOBA_TOPK:
            g = gate_t[:, qs]
            rank = jnp.zeros(g.shape, F32)
            for c in range(qb):
                gc = g[c:c + 1, :]
                beats = (gc > g) | ((gc == g) & (c < blk_row))
                rank = rank + jnp.where(beats, 1.0, 0.0)
            sel = jnp.where((blk_row < qb) & (rank < MOBA_TOPK), 1.0, 0.0)
        pieces = []
        for n in range(qb + 1):
            sn = s[n * blk:(n + 1) * blk]
            if n == qb:
                sn = jnp.where(key_id <= qry_id, sn, NEG_INF)
            elif qb > MOBA_TOPK:
                sn = jnp.where(sel[n:n + 1, :] > 0.5, sn, NEG_INF)
            pieces.append(sn)
        m = jnp.max(pieces[0], axis=0, keepdims=True)
        for sn in pieces[1:]:
            m = jnp.maximum(m, jnp.max(sn, axis=0, keepdims=True))
        p = [jnp.exp(sn - m) for sn in pieces]
        l = jnp.sum(p[0], axis=0, keepdims=True)
        for pn in p[1:]:
            l = l + jnp.sum(pn, axis=0, keepdims=True)
        p_bf = jnp.concatenate([pn.astype(BF16) for pn in p], axis=0)
        out_t = _dot(vt_bf[:, :n_keys], p_bf) * (1.0 / l)
        o_ref[qs, :] = out_t.T.astype(o_ref.dtype)


def _moba_prompt(qkv, batch, seq):
    n_blk = seq // MOBA_BLOCK
    n_pages = seq // PAGE_SIZE
    hd = HEAD_DIM
    km_rows = -(-n_blk // V7X_SUBLANES) * V7X_SUBLANES
    col = lambda part: pl.BlockSpec((seq, hd), lambda b, h: (b, part * N_HEADS + h))
    page_spec = pl.BlockSpec((None, n_pages, None, None, PAGE_SIZE, hd), lambda b, h: (b, 0, 0, h, 0, 0))
    page_shape = jax.ShapeDtypeStruct((batch, n_pages, 1, N_HEADS, PAGE_SIZE, hd), F32)
    return pl.pallas_call(
        functools.partial(_moba_prompt_kernel, n_blk=n_blk),
        grid=(batch, N_HEADS),
        in_specs=[col(0), col(1), col(2)],
        out_specs=[pl.BlockSpec((seq, hd), lambda b, h: (b, h)), page_spec, page_spec],
        out_shape=[jax.ShapeDtypeStruct((batch * seq, D_MODEL), BF16), page_shape, page_shape],
        scratch_shapes=[pltpu.VMEM((km_rows, hd), F32)],
        compiler_params=_params(2),
        name="moba_prompt",
    )(qkv, qkv, qkv)


_BATCH_QK = functools.partial(lax.dot_general, dimension_numbers=_BATCH_CONTRACT_LAST, preferred_element_type=F32)
_BATCH_PV = functools.partial(lax.dot_general, dimension_numbers=_BATCH_MATMUL, preferred_element_type=F32)

_DEC_PAGES_PER_STEP = 4 * PAGES_PER_BLOCK


def _page_specs(n_heads, layer):
    def spec(which):
        return pl.BlockSpec((None, None, n_heads, PAGE_SIZE, HEAD_DIM),
                            lambda i, j, pt: (pt[i, _DEC_PAGES_PER_STEP * j + which], layer, 0, 0, 0))
    return [spec(w) for w in range(_DEC_PAGES_PER_STEP)]


def _dec_k_pass_kernel(pt_ref, q_ref, *refs):
    del pt_ref
    k_refs = refs[:_DEC_PAGES_PER_STEP]
    km_ref, s_ref = refs[_DEC_PAGES_PER_STEP:]
    q_bf = q_ref[...].astype(BF16)
    sums = []
    for pg, k_ref in enumerate(k_refs):
        k = k_ref[...]
        sums.append(jnp.sum(k, axis=1))
        s_ref[:, :, pg * PAGE_SIZE:(pg + 1) * PAGE_SIZE] = _BATCH_QK(q_bf, k.astype(BF16)) * ATTN_SCALE
    for b in range(_DEC_PAGES_PER_STEP // PAGES_PER_BLOCK):
        tot = sums[b * PAGES_PER_BLOCK]
        for pg in range(1, PAGES_PER_BLOCK):
            tot = tot + sums[b * PAGES_PER_BLOCK + pg]
        km_ref[b] = tot / MOBA_BLOCK


def _dec_k_pass(q, cache_k, page_table, layer):
    n, h, t8, hd = q.shape
    n_pages = page_table.shape[1]
    n_blocks = n_pages // PAGES_PER_BLOCK
    blocks_per_step = _DEC_PAGES_PER_STEP // PAGES_PER_BLOCK
    keys_per_step = _DEC_PAGES_PER_STEP * PAGE_SIZE
    return pl.pallas_call(
        _dec_k_pass_kernel,
        grid_spec=pltpu.PrefetchScalarGridSpec(
            num_scalar_prefetch=1, grid=(n, n_pages // _DEC_PAGES_PER_STEP),
            in_specs=[pl.BlockSpec((None, h, t8, hd), lambda i, j, pt: (i, 0, 0, 0))] + _page_specs(h, layer),
            out_specs=[pl.BlockSpec((None, blocks_per_step, h, hd), lambda i, j, pt: (i, j, 0, 0)),
                       pl.BlockSpec((None, h, t8, keys_per_step), lambda i, j, pt: (i, 0, 0, j))]),
        out_shape=[jax.ShapeDtypeStruct((n, n_blocks, h, hd), F32),
                   jax.ShapeDtypeStruct((n, h, t8, n_pages * PAGE_SIZE), F32)],
        compiler_params=_params(2),
        name="dec_k_pass",
    )(page_table, q, *([cache_k] * _DEC_PAGES_PER_STEP))


def _dec_softmax_kernel(q_ref, km_ref, s_ref, kn_ref, p_ref, pown_ref, *, n_blocks, t_new):
    q_hi, q_lo = _split_bf16(q_ref[...])
    km_hi, km_lo = _split_bf16(km_ref[...])
    gate = _BATCH_QK(q_hi, km_hi) + _BATCH_QK(q_lo, km_hi) + _BATCH_QK(q_hi, km_lo)
    lane = lax.broadcasted_iota(jnp.int32, gate.shape, 2)
    sel = jnp.where(_topk_mask(gate, n_blocks, lane, n_blocks), 1.0, 0.0)

    t_id = lax.broadcasted_iota(jnp.int32, (gate.shape[0], gate.shape[1], 1), 1)
    qf = q_hi.astype(F32)
    kf = kn_ref[...].astype(BF16).astype(F32)
    s_own = [jnp.where(c <= t_id, jnp.sum(qf * kf[:, c:c + 1, :], axis=-1, keepdims=True) * ATTN_SCALE, NEG_INF)
             for c in range(t_new)]
    m = s_own[0]
    for c in range(1, t_new):
        m = jnp.maximum(m, s_own[c])
    pieces = []
    for b in range(n_blocks):
        sb = jnp.where(sel[:, :, b:b + 1] > 0.5, s_ref[:, :, b * MOBA_BLOCK:(b + 1) * MOBA_BLOCK], NEG_INF)
        pieces.append(sb)
        m = jnp.maximum(m, jnp.max(sb, axis=-1, keepdims=True))
    e_own = [jnp.exp(s - m) for s in s_own]
    l = e_own[0]
    for e in e_own[1:]:
        l = l + e
    e_past = [jnp.exp(sb - m) for sb in pieces]
    for e in e_past:
        l = l + jnp.sum(e, axis=-1, keepdims=True)
    for b, e in enumerate(e_past):
        p_ref[:, :, b * MOBA_BLOCK:(b + 1) * MOBA_BLOCK] = e / l
    own = jnp.zeros(pown_ref.shape, F32)
    own_lane = lax.broadcasted_iota(jnp.int32, pown_ref.shape, 2)
    for c in range(t_new):
        own = jnp.where(own_lane == c, e_own[c] / l, own)
    pown_ref[...] = own


def _dec_softmax(q, k_mean, scores, k_new, n_blocks, t_new):
    n, h, t8, hd = q.shape
    past = scores.shape[-1]
    spec = lambda rows, cols: pl.BlockSpec((None, h, rows, cols), lambda i: (i, 0, 0, 0))
    return pl.pallas_call(
        functools.partial(_dec_softmax_kernel, n_blocks=n_blocks, t_new=t_new),
        grid=(n,),
        in_specs=[spec(t8, hd), spec(V7X_LANES, hd), spec(t8, past), spec(t8, hd)],
        out_specs=[spec(t8, past), spec(t8, V7X_LANES)],
        out_shape=[jax.ShapeDtypeStruct((n, h, t8, past), F32), jax.ShapeDtypeStruct((n, h, t8, V7X_LANES), F32)],
        compiler_params=_params(1),
        name="dec_softmax",
    )(q, k_mean, scores, k_new)


def _dec_v_pass_kernel(pt_ref, p_ref, pown_ref, vn_ref, *refs, t_new):
    del pt_ref
    v_refs = refs[:_DEC_PAGES_PER_STEP]
    o_ref = refs[_DEC_PAGES_PER_STEP]

    @pl.when(pl.program_id(1) == 0)
    def _():
        p_own = pown_ref[...].astype(BF16).astype(F32)
        vf = vn_ref[...].astype(BF16).astype(F32)
        acc = jnp.zeros(o_ref.shape, F32)
        for c in range(t_new):
            acc = acc + p_own[:, :, c:c + 1] * vf[:, c:c + 1, :]
        o_ref[...] = acc

    tot = o_ref[...]
    for pg, v_ref in enumerate(v_refs):
        p = p_ref[:, :, pg * PAGE_SIZE:(pg + 1) * PAGE_SIZE].astype(BF16)
        tot = tot + _BATCH_PV(p, v_ref[...].astype(BF16))
    o_ref[...] = tot


def _dec_v_pass(probs, p_own, v_new, cache_v, page_table, layer, t_new):
    n, h, t8, hd = v_new.shape
    n_pages = page_table.shape[1]
    keys_per_step = _DEC_PAGES_PER_STEP * PAGE_SIZE
    tok = lambda cols: pl.BlockSpec((None, h, t8, cols), lambda i, j, pt: (i, 0, 0, 0))
    return pl.pallas_call(
        functools.partial(_dec_v_pass_kernel, t_new=t_new),
        grid_spec=pltpu.PrefetchScalarGridSpec(
            num_scalar_prefetch=1, grid=(n, n_pages // _DEC_PAGES_PER_STEP),
            in_specs=[pl.BlockSpec((None, h, t8, keys_per_step), lambda i, j, pt: (i, 0, 0, j)),
                      tok(V7X_LANES), tok(hd)] + _page_specs(h, layer),
            out_specs=tok(hd)),
        out_shape=jax.ShapeDtypeStruct((n, h, t8, hd), F32),
        compiler_params=_params(2),
        name="dec_v_pass",
    )(page_table, probs, p_own, v_new, *([cache_v] * _DEC_PAGES_PER_STEP))


def _top_values(s, k):
    vals = []
    for r in range(k):
        m = jnp.max(s, axis=0, keepdims=True)
        vals.append(m)
        if r + 1 < k:
            s = jnp.where(s == m, NEG_INF, s)
    return vals


_PEER_CANDIDATES = [(a, b) for a in range(PEER_TOPK) for b in range(PEER_TOPK // (a + 1))]
_PEER_CAND_ROWS = -(-len(_PEER_CANDIDATES) // V7X_SUBLANES) * V7X_SUBLANES


def _peer_route_kernel(x_ref, wq_ref, sk_ref, s2_ref, e2_ref, tau_ref, e1_ref, cand_ref):
    q = _dot(x_ref[...].astype(BF16), wq_ref[...]).astype(BF16)
    cand_ref[...] = jnp.full(cand_ref.shape, NEG_INF, F32)
    for h in range(PEER_HEADS):
        s = []
        for p in range(2):
            c0 = (2 * h + p) * PEER_HALF
            s.append(_dot_nt(sk_ref[h, p], q[:, c0:c0 + PEER_HALF]))
        v1 = _top_values(s[0], PEER_TOPK)
        v2 = _top_values(s[1], PEER_TOPK)
        for r, (a, b) in enumerate(_PEER_CANDIDATES):
            cand_ref[r:r + 1, :] = v1[a] + v2[b]
        cand = cand_ref[...]
        thr = _top_values(cand, PEER_TOPK)[-1]
        z = jnp.sum(jnp.where(cand >= thr, jnp.exp(cand - (v1[0] + v2[0])), 0.0), axis=0, keepdims=True)
        s2_ref[h] = s[1]
        e2_ref[h] = jnp.exp(s[1] - v2[0])
        tau_ref[h] = thr - s[0]
        e1_ref[h] = 0.5 * jnp.exp(s[0] - v1[0]) / z


def _peer_route(x, w_q, sub_keys, tm):
    m, d = x.shape
    out_spec = pl.BlockSpec((PEER_HEADS, PEER_NKEYS, tm), lambda i: (0, 0, i))
    out_shape = jax.ShapeDtypeStruct((PEER_HEADS, PEER_NKEYS, m), F32)
    return pl.pallas_call(
        _peer_route_kernel,
        grid=(m // tm,),
        in_specs=[pl.BlockSpec((tm, d), lambda i: (i, 0)),
                  pl.BlockSpec(w_q.shape, lambda i: (0, 0)),
                  pl.BlockSpec(sub_keys.shape, lambda i: (0, 0, 0, 0))],
        out_specs=[out_spec] * 4,
        out_shape=[out_shape] * 4,
        scratch_shapes=[pltpu.VMEM((_PEER_CAND_ROWS, tm), F32)],
        compiler_params=_params(1),
        name="peer_route",
    )(x, w_q, sub_keys)


def _gelu_x2(x):
    return x * (1.0 + lax.erf(x * (2.0 ** -0.5)))


def _peer_expert_kernel(x_ref, u_ref, v_ref, s2_ref, e2_ref, tau_ref, e1_ref, g_ref, b_ref,
                        o_ref, xb_ref, st_ref, aw_ref, *, te, tm):
    c = pl.program_id(1)

    @pl.when(c == 0)
    def _():
        o_ref[...] = jnp.zeros(o_ref.shape, F32)
        xb_ref[...] = x_ref[...].astype(BF16)

    st_ref[...] = _dot_nt(u_ref[...], xb_ref[...])

    rows_per_chunk = te // PEER_NKEYS
    chunks_per_group = max(V7X_SUBLANES // rows_per_chunk, 1)
    group = pl.multiple_of((c * rows_per_chunk // V7X_SUBLANES) * V7X_SUBLANES, V7X_SUBLANES)
    part = c % chunks_per_group

    def row_of(ref, h, r, cols):
        base = group + (r // V7X_SUBLANES) * V7X_SUBLANES
        x8 = ref[h, pl.ds(base, V7X_SUBLANES), cols]
        r8 = r % V7X_SUBLANES
        row = x8[r8:r8 + 1, :]
        for s in range(1, chunks_per_group):
            o = s * rows_per_chunk + r8
            row = jnp.where(part == s, x8[o:o + 1, :], row)
        return row

    for r in range(rows_per_chunk):
        rows = slice(r * PEER_NKEYS, (r + 1) * PEER_NKEYS)
        for tb in range(tm // V7X_LANES):
            cols = slice(tb * V7X_LANES, (tb + 1) * V7X_LANES)
            w = None
            for h in range(PEER_HEADS):
                tau = row_of(tau_ref, h, r, cols)
                e1 = row_of(e1_ref, h, r, cols)
                wh = jnp.where(s2_ref[h, :, cols] >= tau, e2_ref[h, :, cols] * e1, 0.0)
                w = wh if w is None else w + wh
            aw_ref[rows, cols] = (_gelu_x2(st_ref[rows, cols]) * w).astype(BF16)
    o_ref[...] += lax.dot_general(aw_ref[...], v_ref[...], _CONTRACT_FIRST, preferred_element_type=F32)

    @pl.when(c == pl.num_programs(1) - 1)
    def _():
        o_ref[...] = _layer_norm(DEEPNORM_ALPHA * x_ref[...] + o_ref[...], g_ref[...], b_ref[...])


def _peer_experts(x, u, v, layer, route, g, b, tm, te):
    m, d = x.shape
    n_exp = u.shape[1]
    tok = pl.BlockSpec((tm, d), lambda i, c: (i, 0))
    tab = pl.BlockSpec((None, te, d), lambda i, c: (layer, c, 0))
    rt = pl.BlockSpec((PEER_HEADS, PEER_NKEYS, tm), lambda i, c: (0, 0, i))
    vec = pl.BlockSpec((1, d), lambda i, c: (0, 0))
    return pl.pallas_call(
        functools.partial(_peer_expert_kernel, te=te, tm=tm),
        grid=(m // tm, n_exp // te),
        in_specs=[tok, tab, tab, rt, rt, rt, rt, vec, vec],
        out_specs=tok,
        out_shape=jax.ShapeDtypeStruct((m, d), F32),
        scratch_shapes=[pltpu.VMEM((tm, d), BF16), pltpu.VMEM((te, tm), F32), pltpu.VMEM((te, tm), BF16)],
        compiler_params=_params(2),
        name="peer_experts",
    )(x, u, v, *route, g, b)


def _peer_layer(x, w_q, sub_keys, u, v, layer, g, b, tm, te):
    route = _peer_route(x, w_q, sub_keys, tm)
    return _peer_experts(x, u, v, layer, route, g, b, tm, te)


def _row_tile(m, pref):
    return min(m, pref)


def kernel(x_prompt, x_sample, cache_k, cache_v, state_conv, page_table, attn_w_qkv, attn_w_o,
           conv_w_in, conv_w, conv_w_out, ln_mix_g, ln_mix_b, ln_ffn_g, ln_ffn_b,
           peer_w_q, peer_sub_keys, peer_u, peer_v):
    batch, seq, d = x_prompt.shape
    n_dec, t_new, _ = x_sample.shape
    n_past_pages = page_table.shape[1]
    n_past_blocks = n_past_pages // PAGES_PER_BLOCK
    t8 = V7X_SUBLANES

    hp = x_prompt.reshape(batch * seq, d)
    hs = x_sample.reshape(n_dec * t_new, d)
    vec = lambda a: a.reshape(1, d)
    bf = lambda a: a.astype(BF16)

    u_bf = bf(peer_u)
    v_bf = bf(peer_v)

    def peer(h, layer):
        m = h.shape[0]
        return _peer_layer(h, bf(peer_w_q[layer]), bf(peer_sub_keys[layer]), u_bf, v_bf, layer,
                           vec(ln_ffn_g[layer]), vec(ln_ffn_b[layer]), tm=_row_tile(m, 512), te=1024)

    w_qkv = bf(attn_w_qkv[0])
    w_o = bf(attn_w_o[0])
    qkv_p = _mm(hp, w_qkv, tm=512, tn=1024)
    qkv_s = _mm(hs, w_qkv, tm=hs.shape[0], tn=1024)
    o_p, new_k_prompt, new_v_prompt = _moba_prompt(qkv_p, batch, seq)

    heads = lambda a: a.reshape(n_dec, t_new, N_HEADS, HEAD_DIM).transpose(0, 2, 1, 3)
    q_s, k_s, v_s = [heads(a) for a in jnp.split(qkv_s, 3, axis=-1)]
    pad_t = lambda a: jnp.pad(a, ((0, 0), (0, 0), (0, t8 - t_new), (0, 0)))
    assert n_past_pages % _DEC_PAGES_PER_STEP == 0 and n_past_blocks <= V7X_LANES
    k_mean, scores = _dec_k_pass(pad_t(q_s), cache_k, page_table, 0)
    k_mean = jnp.pad(k_mean.transpose(0, 2, 1, 3), ((0, 0), (0, 0), (0, V7X_LANES - n_past_blocks), (0, 0)))
    probs, p_own = _dec_softmax(pad_t(q_s), k_mean, scores, pad_t(k_s), n_past_blocks, t_new)
    o_s = _dec_v_pass(probs, p_own, pad_t(v_s), cache_v, page_table, 0, t_new)
    o_s = o_s[:, :, :t_new].transpose(0, 2, 1, 3).reshape(n_dec * t_new, d)

    hp = _mm_res_ln(o_p, w_o, hp, vec(ln_mix_g[0]), vec(ln_mix_b[0]), tm=256)
    hs = _mm_res_ln(o_s, w_o, hs, vec(ln_mix_g[0]), vec(ln_mix_b[0]), tm=hs.shape[0])
    hp = peer(hp, 0)
    hs = peer(hs, 0)

    w_in = bf(conv_w_in[0])
    w_out = bf(conv_w_out[0])
    bg_p, z_p = _conv_in(hp, w_in, tm=512, tn=512)
    bg_s, z_s = _conv_in(hs, w_in, tm=hs.shape[0], tn=512)
    hp = _conv_out_prompt(bg_p, z_p, conv_w[0], w_out, hp, vec(ln_mix_g[1]), vec(ln_mix_b[1]), tm=256, seq_len=seq)
    time_major = lambda a: a.reshape(n_dec, t_new, d).transpose(1, 0, 2)
    u_s, conv_state_s = _conv_sample(time_major(bg_s), time_major(z_s), state_conv[0].transpose(1, 0, 2), conv_w[0])
    hs = _mm_res_ln(u_s.transpose(1, 0, 2).reshape(n_dec * t_new, d), w_out, hs,
                    vec(ln_mix_g[1]), vec(ln_mix_b[1]), tm=hs.shape[0])
    hp = peer(hp, 1)
    hs = peer(hs, 1)

    new_k_sample = k_s[:, None]
    new_v_sample = v_s[:, None]
    new_conv_prompt = z_p.reshape(batch, seq, d)[:, seq - (CONV_WIDTH - 1):][None]
    new_conv_sample = conv_state_s.transpose(1, 0, 2)[None]
    return (hp.reshape(batch, seq, d), hs.reshape(n_dec, t_new, d), new_k_prompt, new_v_prompt,
            new_k_sample, new_v_sample, new_conv_prompt, new_conv_sample)
```

```python
import functools

import jax
import jax.numpy as jnp
from jax import lax
from jax.experimental import pallas as pl
from jax.experimental.pallas import tpu as pltpu

F32 = jnp.float32
BF16 = jnp.bfloat16

D_MODEL = 2048
N_HEADS = 16
HEAD_DIM = 128
PAGE_SIZE = 128
MOBA_BLOCK = 256
MOBA_TOPK = 3
PAGES_PER_BLOCK = MOBA_BLOCK // PAGE_SIZE
CONV_WIDTH = 3
PEER_HEADS = 8
PEER_NKEYS = 128
PEER_TOPK = 16
PEER_HALF = 128
LN_EPS = 1e-5
DEPTH = 2
DEEPNORM_ALPHA = (2.0 * DEPTH) ** 0.25
ATTN_SCALE = HEAD_DIM ** -0.5
LOG2_E = 1.4426950408889634
NEG_INF = float("-inf")

V7X_LANES = 128
V7X_SUBLANES = 8
V7X_VMEM_LIMIT_BYTES = 60 * 1024 * 1024

_CONTRACT_LAST = (((1,), (1,)), ((), ()))
_CONTRACT_FIRST = (((0,), (0,)), ((), ()))
_BATCH_CONTRACT_LAST = (((2,), (2,)), ((0,), (0,)))
_BATCH_MATMUL = (((2,), (1,)), ((0,), (0,)))


def _params(n_axes):
    return pltpu.CompilerParams(dimension_semantics=("arbitrary",) * n_axes,
                                vmem_limit_bytes=V7X_VMEM_LIMIT_BYTES)


def _dot(a, b):
    return jnp.dot(a, b, preferred_element_type=F32)


def _dot_nt(a, b):
    return lax.dot_general(a, b, _CONTRACT_LAST, preferred_element_type=F32)


def _split_bf16(x):
    hi = x.astype(BF16)
    lo = (x - hi.astype(F32)).astype(BF16)
    return hi, lo


def _layer_norm(x, g, b):
    mu = jnp.mean(x, axis=-1, keepdims=True)
    xc = x - mu
    var = jnp.mean(xc * xc, axis=-1, keepdims=True)
    return xc * lax.rsqrt(var + LN_EPS) * g + b


def _mm_kernel(x_ref, w_ref, o_ref):
    o_ref[...] = _dot(x_ref[...].astype(BF16), w_ref[...])


def _mm(x, w, tm, tn):
    m, k = x.shape
    n = w.shape[1]
    return pl.pallas_call(
        _mm_kernel,
        grid=(n // tn, m // tm),
        in_specs=[pl.BlockSpec((tm, k), lambda j, i: (i, 0)),
                  pl.BlockSpec((k, tn), lambda j, i: (0, j))],
        out_specs=pl.BlockSpec((tm, tn), lambda j, i: (i, j)),
        out_shape=jax.ShapeDtypeStruct((m, n), F32),
        compiler_params=_params(2),
        name="proj",
    )(x, w)


def _mm_res_ln_kernel(x_ref, w_ref, h_ref, g_ref, b_ref, o_ref):
    y = _dot(x_ref[...].astype(BF16), w_ref[...])
    o_ref[...] = _layer_norm(DEEPNORM_ALPHA * h_ref[...] + y, g_ref[...], b_ref[...])


def _mm_res_ln(x, w, h, g, b, tm):
    m, k = x.shape
    d = w.shape[1]
    row = lambda i: (i, 0)
    fixed = lambda i: (0, 0)
    return pl.pallas_call(
        _mm_res_ln_kernel,
        grid=(m // tm,),
        in_specs=[pl.BlockSpec((tm, k), row), pl.BlockSpec((k, d), fixed),
                  pl.BlockSpec((tm, d), row), pl.BlockSpec((1, d), fixed), pl.BlockSpec((1, d), fixed)],
        out_specs=pl.BlockSpec((tm, d), row),
        out_shape=jax.ShapeDtypeStruct((m, d), F32),
        compiler_params=_params(1),
        name="out_proj_ln",
    )(x, w, h, g, b)


def _conv_in_kernel(x_ref, wb_ref, wc_ref, wh_ref, bg_ref, z_ref):
    xb = x_ref[...].astype(BF16)
    bg_ref[...] = _dot(xb, wb_ref[...])
    z_ref[...] = _dot(xb, wc_ref[...]) * _dot(xb, wh_ref[...])


def _conv_in(x, w_in, tm, tn):
    m, k = x.shape
    d = w_in.shape[1] // 3
    nb = d // tn
    xs = pl.BlockSpec((tm, k), lambda j, i: (i, 0))
    ws = [pl.BlockSpec((k, tn), functools.partial(lambda j, i, off: (0, j + off), off=part * nb))
          for part in range(3)]
    os_ = pl.BlockSpec((tm, tn), lambda j, i: (i, j))
    return pl.pallas_call(
        _conv_in_kernel,
        grid=(nb, m // tm),
        in_specs=[xs] + ws,
        out_specs=[os_, os_],
        out_shape=[jax.ShapeDtypeStruct((m, d), F32)] * 2,
        compiler_params=_params(2),
        name="conv_in",
    )(x, w_in, w_in, w_in)


def _conv_out_prompt_kernel(bg_ref, z_ref, zp_ref, cw_ref, w_ref, h_ref, g_ref, b_ref, o_ref, *, tiles_per_seq):
    i = pl.program_id(0)
    z = z_ref[...]
    seq_start = (i % tiles_per_seq) == 0
    zp = jnp.where(seq_start, 0.0, zp_ref[...])
    row = lax.broadcasted_iota(jnp.int32, z.shape, 0)
    last = V7X_SUBLANES - 1
    z1 = jnp.where(row == 0, zp[last:last + 1, :], pltpu.roll(z, 1, 0))
    z2 = jnp.where(row == 0, zp[last - 1:last, :],
                   jnp.where(row == 1, zp[last:last + 1, :], pltpu.roll(z, 2, 0)))
    cw = cw_ref[...]
    y = cw[0:1, :] * z2 + cw[1:2, :] * z1 + cw[2:3, :] * z
    u = (bg_ref[...] * y).astype(BF16)
    o_ref[...] = _layer_norm(DEEPNORM_ALPHA * h_ref[...] + _dot(u, w_ref[...]), g_ref[...], b_ref[...])


def _conv_out_prompt(bg, z, conv_w, w_out, h, g, b, tm, seq_len):
    m, d = z.shape
    row = lambda i: (i, 0)
    fixed = lambda i: (0, 0)
    halo = lambda i: (jnp.maximum(i * (tm // V7X_SUBLANES) - 1, 0), 0)
    return pl.pallas_call(
        functools.partial(_conv_out_prompt_kernel, tiles_per_seq=seq_len // tm),
        grid=(m // tm,),
        in_specs=[pl.BlockSpec((tm, d), row), pl.BlockSpec((tm, d), row),
                  pl.BlockSpec((V7X_SUBLANES, d), halo), pl.BlockSpec((CONV_WIDTH, d), fixed),
                  pl.BlockSpec((d, d), fixed), pl.BlockSpec((tm, d), row),
                  pl.BlockSpec((1, d), fixed), pl.BlockSpec((1, d), fixed)],
        out_specs=pl.BlockSpec((tm, d), row),
        out_shape=jax.ShapeDtypeStruct((m, d), F32),
        compiler_params=_params(1),
        name="conv_out_prompt",
    )(bg, z, z, conv_w, w_out, h, g, b)


def _conv_sample_kernel(bg_ref, z_ref, st_ref, cw_ref, u_ref, ns_ref, *, t):
    cw = cw_ref[...]
    zp = [st_ref[j] for j in range(CONV_WIDTH - 1)] + [z_ref[j] for j in range(t)]
    for j in range(t):
        y = cw[0:1, :] * zp[j] + cw[1:2, :] * zp[j + 1] + cw[2:3, :] * zp[j + 2]
        u_ref[j] = bg_ref[j] * y
    for j in range(CONV_WIDTH - 1):
        ns_ref[j] = zp[t + j]


def _conv_sample(bg, z, state, conv_w):
    t, n, d = z.shape
    return pl.pallas_call(
        functools.partial(_conv_sample_kernel, t=t),
        out_shape=[jax.ShapeDtypeStruct((t, n, d), F32), jax.ShapeDtypeStruct((CONV_WIDTH - 1, n, d), F32)],
        name="conv_sample",
    )(bg, z, state, conv_w)


def _topk_mask(g, n_valid, lane, n_cand):
    rank = jnp.zeros(g.shape, F32)
    for c in range(n_cand):
        col = g[..., c:c + 1]
        beats = (col > g) | ((col == g) & (c < lane))
        rank = rank + jnp.where(beats & (c < n_valid), 1.0, 0.0)
    return (lane < n_valid) & (rank < MOBA_TOPK)


def _side_cast_specs(tables, layer, n_steps, step_index):
    in_specs, out_specs, out_shapes = [], [], []
    for t in tables:
        _, rows, cols = t.shape
        slab = rows // n_steps
        assert slab * n_steps == rows and slab % (2 * V7X_SUBLANES) == 0
        in_specs.append(pl.BlockSpec((None, slab, cols), lambda *g: (layer, step_index(*g), 0)))
        out_specs.append(pl.BlockSpec((slab, cols), lambda *g: (step_index(*g), 0)))
        out_shapes.append(jax.ShapeDtypeStruct((rows, cols), BF16))
    return in_specs, out_specs, out_shapes


def _side_cast(src_refs, dst_refs):
    for src, dst in zip(src_refs, dst_refs):
        dst[...] = src[...].astype(dst.dtype)


def _moba_prompt_kernel(q_ref, k_ref, v_ref, *refs, n_blk, n_side):
    side_in, (o_ref, kp_ref, vp_ref) = refs[:n_side], refs[n_side:n_side + 3]
    side_out, km_ref = refs[n_side + 3:2 * n_side + 3], refs[2 * n_side + 3]
    _side_cast(side_in, side_out)
    blk = MOBA_BLOCK
    k = k_ref[...]
    v = v_ref[...]
    kp_ref[...] = k.reshape(kp_ref.shape)
    vp_ref[...] = v.reshape(vp_ref.shape)
    k_bf = k.astype(BF16)
    vt_bf = v.T.astype(BF16)
    km_ref[...] = jnp.zeros(km_ref.shape, F32)
    for n in range(n_blk):
        km_ref[n:n + 1, :] = jnp.mean(k[n * blk:(n + 1) * blk, :], axis=0, keepdims=True)
    q_hi, q_lo = _split_bf16(q_ref[...])
    km_hi, km_lo = _split_bf16(km_ref[...])
    gate_t = _dot_nt(km_hi, q_hi) + _dot_nt(km_hi, q_lo) + _dot_nt(km_lo, q_hi)
    blk_row = lax.broadcasted_iota(jnp.int32, (km_ref.shape[0], blk), 0)
    key_id = lax.broadcasted_iota(jnp.int32, (blk, blk), 0)
    qry_id = lax.broadcasted_iota(jnp.int32, (blk, blk), 1)

    for qb in range(n_blk):
        qs = slice(qb * blk, (qb + 1) * blk)
        n_keys = (qb + 1) * blk
        s = _dot_nt(k_bf[:n_keys], q_hi[qs]) * (ATTN_SCALE * LOG2_E)
        if qb > MOBA_TOPK:
            g = gate_t[:, qs]
            rank = jnp.zeros(g.shape, F32)
            for c in range(qb):
                gc = g[c:c + 1, :]
                beats = (gc > g) | ((gc == g) & (c < blk_row))
                rank = rank + jnp.where(beats, 1.0, 0.0)
            sel = jnp.where((blk_row < qb) & (rank < MOBA_TOPK), 1.0, 0.0)
        pieces = []
        for n in range(qb + 1):
            sn = s[n * blk:(n + 1) * blk]
            if n == qb:
                sn = jnp.where(key_id <= qry_id, sn, NEG_INF)
            elif qb > MOBA_TOPK:
                sn = jnp.where(sel[n:n + 1, :] > 0.5, sn, NEG_INF)
            pieces.append(sn)
        m = jnp.max(pieces[0], axis=0, keepdims=True)
        for sn in pieces[1:]:
            m = jnp.maximum(m, jnp.max(sn, axis=0, keepdims=True))
        p = [jnp.exp2(sn - m) for sn in pieces]
        l = jnp.sum(p[0], axis=0, keepdims=True)
        for pn in p[1:]:
            l = l + jnp.sum(pn, axis=0, keepdims=True)
        p_bf = jnp.concatenate([pn.astype(BF16) for pn in p], axis=0)
        out_t = _dot(vt_bf[:, :n_keys], p_bf) * (1.0 / l)
        o_ref[qs, :] = out_t.T.astype(o_ref.dtype)


def _moba_prompt(qkv, batch, seq, cast_tables=(), cast_layer=0):
    n_blk = seq // MOBA_BLOCK
    n_pages = seq // PAGE_SIZE
    hd = HEAD_DIM
    km_rows = -(-n_blk // V7X_SUBLANES) * V7X_SUBLANES
    col = lambda part: pl.BlockSpec((seq, hd), lambda b, h: (b, part * N_HEADS + h))
    page_spec = pl.BlockSpec((None, n_pages, None, None, PAGE_SIZE, hd), lambda b, h: (b, 0, 0, h, 0, 0))
    page_shape = jax.ShapeDtypeStruct((batch, n_pages, 1, N_HEADS, PAGE_SIZE, hd), F32)
    side_in, side_out, side_shapes = _side_cast_specs(cast_tables, cast_layer, batch * N_HEADS,
                                                      lambda b, h: b * N_HEADS + h)
    return pl.pallas_call(
        functools.partial(_moba_prompt_kernel, n_blk=n_blk, n_side=len(cast_tables)),
        grid=(batch, N_HEADS),
        in_specs=[col(0), col(1), col(2)] + side_in,
        out_specs=[pl.BlockSpec((seq, hd), lambda b, h: (b, h)), page_spec, page_spec] + side_out,
        out_shape=[jax.ShapeDtypeStruct((batch * seq, D_MODEL), BF16), page_shape, page_shape] + side_shapes,
        scratch_shapes=[pltpu.VMEM((km_rows, hd), F32)],
        compiler_params=_params(2),
        name="moba_prompt",
    )(qkv, qkv, qkv, *cast_tables)


_BATCH_QK = functools.partial(lax.dot_general, dimension_numbers=_BATCH_CONTRACT_LAST, preferred_element_type=F32)
_BATCH_PV = functools.partial(lax.dot_general, dimension_numbers=_BATCH_MATMUL, preferred_element_type=F32)

_DEC_PAGES_PER_STEP = 8 * PAGES_PER_BLOCK


def _page_specs(n_heads, layer):
    def spec(which):
        return pl.BlockSpec((None, None, n_heads, PAGE_SIZE, HEAD_DIM),
                            lambda i, j, pt: (pt[i, _DEC_PAGES_PER_STEP * j + which], layer, 0, 0, 0))
    return [spec(w) for w in range(_DEC_PAGES_PER_STEP)]


def _dec_k_pass_kernel(pt_ref, q_ref, *refs):
    del pt_ref
    k_refs = refs[:_DEC_PAGES_PER_STEP]
    km_ref, s_ref = refs[_DEC_PAGES_PER_STEP:]
    q_bf = q_ref[...].astype(BF16)
    sums = []
    for pg, k_ref in enumerate(k_refs):
        k = k_ref[...]
        sums.append(jnp.sum(k, axis=1))
        s_ref[:, :, pg * PAGE_SIZE:(pg + 1) * PAGE_SIZE] = _BATCH_QK(q_bf, k.astype(BF16)) * ATTN_SCALE
    for b in range(_DEC_PAGES_PER_STEP // PAGES_PER_BLOCK):
        tot = sums[b * PAGES_PER_BLOCK]
        for pg in range(1, PAGES_PER_BLOCK):
            tot = tot + sums[b * PAGES_PER_BLOCK + pg]
        km_ref[b] = tot / MOBA_BLOCK


def _dec_k_pass(q, cache_k, page_table, layer):
    n, h, t8, hd = q.shape
    n_pages = page_table.shape[1]
    n_blocks = n_pages // PAGES_PER_BLOCK
    blocks_per_step = _DEC_PAGES_PER_STEP // PAGES_PER_BLOCK
    keys_per_step = _DEC_PAGES_PER_STEP * PAGE_SIZE
    return pl.pallas_call(
        _dec_k_pass_kernel,
        grid_spec=pltpu.PrefetchScalarGridSpec(
            num_scalar_prefetch=1, grid=(n, n_pages // _DEC_PAGES_PER_STEP),
            in_specs=[pl.BlockSpec((None, h, t8, hd), lambda i, j, pt: (i, 0, 0, 0))] + _page_specs(h, layer),
            out_specs=[pl.BlockSpec((None, blocks_per_step, h, hd), lambda i, j, pt: (i, j, 0, 0)),
                       pl.BlockSpec((None, h, t8, keys_per_step), lambda i, j, pt: (i, 0, 0, j))]),
        out_shape=[jax.ShapeDtypeStruct((n, n_blocks, h, hd), F32),
                   jax.ShapeDtypeStruct((n, h, t8, n_pages * PAGE_SIZE), F32)],
        compiler_params=_params(2),
        name="dec_k_pass",
    )(page_table, q, *([cache_k] * _DEC_PAGES_PER_STEP))


def _dec_softmax_kernel(q_ref, km_ref, s_ref, kn_ref, p_ref, pown_ref, *, n_blocks, t_new):
    q_hi, q_lo = _split_bf16(q_ref[...])
    km_hi, km_lo = _split_bf16(km_ref[...])
    gate = _BATCH_QK(q_hi, km_hi) + _BATCH_QK(q_lo, km_hi) + _BATCH_QK(q_hi, km_lo)
    lane = lax.broadcasted_iota(jnp.int32, gate.shape, 2)
    sel = jnp.where(_topk_mask(gate, n_blocks, lane, n_blocks), 1.0, 0.0)

    t_id = lax.broadcasted_iota(jnp.int32, (gate.shape[0], gate.shape[1], 1), 1)
    qf = q_hi.astype(F32)
    kf = kn_ref[...].astype(BF16).astype(F32)
    s_own = [jnp.where(c <= t_id, jnp.sum(qf * kf[:, c:c + 1, :], axis=-1, keepdims=True) * ATTN_SCALE, NEG_INF)
             for c in range(t_new)]
    m = s_own[0]
    for c in range(1, t_new):
        m = jnp.maximum(m, s_own[c])
    pieces = []
    for b in range(n_blocks):
        sb = jnp.where(sel[:, :, b:b + 1] > 0.5, s_ref[:, :, b * MOBA_BLOCK:(b + 1) * MOBA_BLOCK], NEG_INF)
        pieces.append(sb)
        m = jnp.maximum(m, jnp.max(sb, axis=-1, keepdims=True))
    e_own = [jnp.exp(s - m) for s in s_own]
    l = e_own[0]
    for e in e_own[1:]:
        l = l + e
    e_past = [jnp.exp(sb - m) for sb in pieces]
    for e in e_past:
        l = l + jnp.sum(e, axis=-1, keepdims=True)
    for b, e in enumerate(e_past):
        p_ref[:, :, b * MOBA_BLOCK:(b + 1) * MOBA_BLOCK] = e / l
    own = jnp.zeros(pown_ref.shape, F32)
    own_lane = lax.broadcasted_iota(jnp.int32, pown_ref.shape, 2)
    for c in range(t_new):
        own = jnp.where(own_lane == c, e_own[c] / l, own)
    pown_ref[...] = own


def _dec_softmax(q, k_mean, scores, k_new, n_blocks, t_new):
    n, h, t8, hd = q.shape
    past = scores.shape[-1]
    spec = lambda rows, cols: pl.BlockSpec((None, h, rows, cols), lambda i: (i, 0, 0, 0))
    return pl.pallas_call(
        functools.partial(_dec_softmax_kernel, n_blocks=n_blocks, t_new=t_new),
        grid=(n,),
        in_specs=[spec(t8, hd), spec(V7X_LANES, hd), spec(t8, past), spec(t8, hd)],
        out_specs=[spec(t8, past), spec(t8, V7X_LANES)],
        out_shape=[jax.ShapeDtypeStruct((n, h, t8, past), F32), jax.ShapeDtypeStruct((n, h, t8, V7X_LANES), F32)],
        compiler_params=_params(1),
        name="dec_softmax",
    )(q, k_mean, scores, k_new)


def _dec_v_pass_kernel(pt_ref, p_ref, pown_ref, vn_ref, *refs, t_new):
    del pt_ref
    v_refs = refs[:_DEC_PAGES_PER_STEP]
    o_ref = refs[_DEC_PAGES_PER_STEP]

    @pl.when(pl.program_id(1) == 0)
    def _():
        p_own = pown_ref[...].astype(BF16).astype(F32)
        vf = vn_ref[...].astype(BF16).astype(F32)
        acc = jnp.zeros(o_ref.shape, F32)
        for c in range(t_new):
            acc = acc + p_own[:, :, c:c + 1] * vf[:, c:c + 1, :]
        o_ref[...] = acc

    tot = o_ref[...]
    for pg, v_ref in enumerate(v_refs):
        p = p_ref[:, :, pg * PAGE_SIZE:(pg + 1) * PAGE_SIZE].astype(BF16)
        tot = tot + _BATCH_PV(p, v_ref[...].astype(BF16))
    o_ref[...] = tot


def _dec_v_pass(probs, p_own, v_new, cache_v, page_table, layer, t_new):
    n, h, t8, hd = v_new.shape
    n_pages = page_table.shape[1]
    keys_per_step = _DEC_PAGES_PER_STEP * PAGE_SIZE
    tok = lambda cols: pl.BlockSpec((None, h, t8, cols), lambda i, j, pt: (i, 0, 0, 0))
    return pl.pallas_call(
        functools.partial(_dec_v_pass_kernel, t_new=t_new),
        grid_spec=pltpu.PrefetchScalarGridSpec(
            num_scalar_prefetch=1, grid=(n, n_pages // _DEC_PAGES_PER_STEP),
            in_specs=[pl.BlockSpec((None, h, t8, keys_per_step), lambda i, j, pt: (i, 0, 0, j)),
                      tok(V7X_LANES), tok(hd)] + _page_specs(h, layer),
            out_specs=tok(hd)),
        out_shape=jax.ShapeDtypeStruct((n, h, t8, hd), F32),
        compiler_params=_params(2),
        name="dec_v_pass",
    )(page_table, probs, p_own, v_new, *([cache_v] * _DEC_PAGES_PER_STEP))


def _top_values(s, k):
    vals = []
    for r in range(k):
        m = jnp.max(s, axis=0, keepdims=True)
        vals.append(m)
        if r + 1 < k:
            s = jnp.where(s == m, NEG_INF, s)
    return vals


_PEER_CANDIDATES = [(a, b) for a in range(PEER_TOPK) for b in range(PEER_TOPK // (a + 1))]
_PEER_CAND_ROWS = -(-len(_PEER_CANDIDATES) // V7X_SUBLANES) * V7X_SUBLANES


def _peer_route_kernel(x_ref, wq_ref, sk_ref, s2_ref, e2_ref, tau_ref, e1_ref, cand_ref):
    q = _dot(x_ref[...].astype(BF16), wq_ref[...]).astype(BF16)
    cand_ref[...] = jnp.full(cand_ref.shape, NEG_INF, F32)
    for h in range(PEER_HEADS):
        s = []
        for p in range(2):
            c0 = (2 * h + p) * PEER_HALF
            s.append(_dot_nt(sk_ref[h, p], q[:, c0:c0 + PEER_HALF]))
        v1 = _top_values(s[0], PEER_TOPK)
        v2 = _top_values(s[1], PEER_TOPK)
        for r, (a, b) in enumerate(_PEER_CANDIDATES):
            cand_ref[r:r + 1, :] = v1[a] + v2[b]
        cand = cand_ref[...]
        thr = _top_values(cand, PEER_TOPK)[-1]
        z = jnp.sum(jnp.where(cand >= thr, jnp.exp(cand - (v1[0] + v2[0])), 0.0), axis=0, keepdims=True)
        s2_ref[h] = s[1]
        e2_ref[h] = jnp.exp(s[1] - v2[0])
        tau_ref[h] = thr - s[0]
        e1_ref[h] = 0.5 * jnp.exp(s[0] - v1[0]) / z


def _peer_route(x, w_q, sub_keys, tm):
    m, d = x.shape
    out_spec = pl.BlockSpec((PEER_HEADS, PEER_NKEYS, tm), lambda i: (0, 0, i))
    out_shape = jax.ShapeDtypeStruct((PEER_HEADS, PEER_NKEYS, m), F32)
    return pl.pallas_call(
        _peer_route_kernel,
        grid=(m // tm,),
        in_specs=[pl.BlockSpec((tm, d), lambda i: (i, 0)),
                  pl.BlockSpec(w_q.shape, lambda i: (0, 0)),
                  pl.BlockSpec(sub_keys.shape, lambda i: (0, 0, 0, 0))],
        out_specs=[out_spec] * 4,
        out_shape=[out_shape] * 4,
        scratch_shapes=[pltpu.VMEM((_PEER_CAND_ROWS, tm), F32)],
        compiler_params=_params(1),
        name="peer_route",
    )(x, w_q, sub_keys)


def _gelu_x2(x):
    return x * (1.0 + lax.erf(x * (2.0 ** -0.5)))


def _peer_expert_kernel(x_ref, u_ref, v_ref, s2_ref, e2_ref, tau_ref, e1_ref, g_ref, b_ref,
                        *refs, te, tm, n_side):
    side_in, o_ref, side_out = refs[:n_side], refs[n_side], refs[n_side + 1:2 * n_side + 1]
    xb_ref, st_ref, aw_ref = refs[2 * n_side + 1:]
    _side_cast(side_in, side_out)
    c = pl.program_id(1)

    @pl.when(c == 0)
    def _():
        o_ref[...] = jnp.zeros(o_ref.shape, F32)
        xb_ref[...] = x_ref[...].astype(BF16)

    st_ref[...] = _dot_nt(u_ref[...], xb_ref[...])

    rows_per_chunk = te // PEER_NKEYS
    chunks_per_group = max(V7X_SUBLANES // rows_per_chunk, 1)
    group = pl.multiple_of((c * rows_per_chunk // V7X_SUBLANES) * V7X_SUBLANES, V7X_SUBLANES)
    part = c % chunks_per_group

    def row_of(ref, h, r, cols):
        base = group + (r // V7X_SUBLANES) * V7X_SUBLANES
        x8 = ref[h, pl.ds(base, V7X_SUBLANES), cols]
        r8 = r % V7X_SUBLANES
        row = x8[r8:r8 + 1, :]
        for s in range(1, chunks_per_group):
            o = s * rows_per_chunk + r8
            row = jnp.where(part == s, x8[o:o + 1, :], row)
        return row

    for r in range(rows_per_chunk):
        rows = slice(r * PEER_NKEYS, (r + 1) * PEER_NKEYS)
        for tb in range(tm // V7X_LANES):
            cols = slice(tb * V7X_LANES, (tb + 1) * V7X_LANES)
            w = None
            for h in range(PEER_HEADS):
                tau = row_of(tau_ref, h, r, cols)
                e1 = row_of(e1_ref, h, r, cols)
                wh = jnp.where(s2_ref[h, :, cols] >= tau, e2_ref[h, :, cols] * e1, 0.0)
                w = wh if w is None else w + wh
            aw_ref[rows, cols] = (_gelu_x2(st_ref[rows, cols]) * w).astype(BF16)
    o_ref[...] += lax.dot_general(aw_ref[...], v_ref[...], _CONTRACT_FIRST, preferred_element_type=F32)

    @pl.when(c == pl.num_programs(1) - 1)
    def _():
        o_ref[...] = _layer_norm(DEEPNORM_ALPHA * x_ref[...] + o_ref[...], g_ref[...], b_ref[...])


def _peer_experts(x, u, v, route, g, b, tm, te, cast_tables=(), cast_layer=0):
    m, d = x.shape
    n_exp = u.shape[0]
    n_chunks = n_exp // te
    tok = pl.BlockSpec((tm, d), lambda i, c: (i, 0))
    tab = pl.BlockSpec((te, d), lambda i, c: (c, 0))
    rt = pl.BlockSpec((PEER_HEADS, PEER_NKEYS, tm), lambda i, c: (0, 0, i))
    vec = pl.BlockSpec((1, d), lambda i, c: (0, 0))
    side_in, side_out, side_shapes = _side_cast_specs(cast_tables, cast_layer, (m // tm) * n_chunks,
                                                      lambda i, c: i * n_chunks + c)
    return pl.pallas_call(
        functools.partial(_peer_expert_kernel, te=te, tm=tm, n_side=len(cast_tables)),
        grid=(m // tm, n_chunks),
        in_specs=[tok, tab, tab, rt, rt, rt, rt, vec, vec] + side_in,
        out_specs=[tok] + side_out,
        out_shape=[jax.ShapeDtypeStruct((m, d), F32)] + side_shapes,
        scratch_shapes=[pltpu.VMEM((tm, d), BF16), pltpu.VMEM((te, tm), F32), pltpu.VMEM((te, tm), BF16)],
        compiler_params=_params(2),
        name="peer_experts",
    )(x, u, v, *route, g, b, *cast_tables)


def _peer_layer(x, w_q, sub_keys, u, v, g, b, tm, te, cast_tables=(), cast_layer=0):
    route = _peer_route(x, w_q, sub_keys, tm)
    return _peer_experts(x, u, v, route, g, b, tm, te, cast_tables, cast_layer)


_PROJ_TILE = (512, 1024)
_CONV_IN_TILE = (512, 512)
_LN_ROWS = 256
_PEER_TILE = (512, 1024)


def _row_tile(m, pref):
    return min(m, pref)


def kernel(x_prompt, x_sample, cache_k, cache_v, state_conv, page_table, attn_w_qkv, attn_w_o,
           conv_w_in, conv_w, conv_w_out, ln_mix_g, ln_mix_b, ln_ffn_g, ln_ffn_b,
           peer_w_q, peer_sub_keys, peer_u, peer_v):
    batch, seq, d = x_prompt.shape
    n_dec, t_new, _ = x_sample.shape
    n_past_pages = page_table.shape[1]
    n_past_blocks = n_past_pages // PAGES_PER_BLOCK
    t8 = V7X_SUBLANES

    hp = x_prompt.reshape(batch * seq, d)
    hs = x_sample.reshape(n_dec * t_new, d)
    vec = lambda a: a.reshape(1, d)
    bf = lambda a: a.astype(BF16)

    expert_tables = (peer_u, peer_v)

    def peer(h, layer, u, v, cast_layer=None):
        m = h.shape[0]
        return _peer_layer(h, bf(peer_w_q[layer]), bf(peer_sub_keys[layer]), u, v,
                           vec(ln_ffn_g[layer]), vec(ln_ffn_b[layer]),
                           tm=_row_tile(m, _PEER_TILE[0]), te=_PEER_TILE[1],
                           cast_tables=() if cast_layer is None else expert_tables, cast_layer=cast_layer or 0)

    w_qkv = bf(attn_w_qkv[0])
    w_o = bf(attn_w_o[0])
    qkv_p = _mm(hp, w_qkv, tm=_PROJ_TILE[0], tn=_PROJ_TILE[1])
    qkv_s = _mm(hs, w_qkv, tm=hs.shape[0], tn=_PROJ_TILE[1])
    o_p, new_k_prompt, new_v_prompt, u0, v0 = _moba_prompt(qkv_p, batch, seq, expert_tables, 0)

    heads = lambda a: a.reshape(n_dec, t_new, N_HEADS, HEAD_DIM).transpose(0, 2, 1, 3)
    q_s, k_s, v_s = [heads(a) for a in jnp.split(qkv_s, 3, axis=-1)]
    pad_t = lambda a: jnp.pad(a, ((0, 0), (0, 0), (0, t8 - t_new), (0, 0)))
    assert n_past_pages % _DEC_PAGES_PER_STEP == 0 and n_past_blocks <= V7X_LANES
    k_mean, scores = _dec_k_pass(pad_t(q_s), cache_k, page_table, 0)
    k_mean = jnp.pad(k_mean.transpose(0, 2, 1, 3), ((0, 0), (0, 0), (0, V7X_LANES - n_past_blocks), (0, 0)))
    probs, p_own = _dec_softmax(pad_t(q_s), k_mean, scores, pad_t(k_s), n_past_blocks, t_new)
    o_s = _dec_v_pass(probs, p_own, pad_t(v_s), cache_v, page_table, 0, t_new)
    o_s = o_s[:, :, :t_new].transpose(0, 2, 1, 3).reshape(n_dec * t_new, d)

    hp = _mm_res_ln(o_p, w_o, hp, vec(ln_mix_g[0]), vec(ln_mix_b[0]), tm=_LN_ROWS)
    hs = _mm_res_ln(o_s, w_o, hs, vec(ln_mix_g[0]), vec(ln_mix_b[0]), tm=hs.shape[0])
    hp, u1, v1 = peer(hp, 0, u0, v0, cast_layer=1)
    hs, = peer(hs, 0, u0, v0)

    w_in = bf(conv_w_in[0])
    w_out = bf(conv_w_out[0])
    bg_p, z_p = _conv_in(hp, w_in, tm=_CONV_IN_TILE[0], tn=_CONV_IN_TILE[1])
    bg_s, z_s = _conv_in(hs, w_in, tm=hs.shape[0], tn=_CONV_IN_TILE[1])
    hp = _conv_out_prompt(bg_p, z_p, conv_w[0], w_out, hp, vec(ln_mix_g[1]), vec(ln_mix_b[1]),
                          tm=_LN_ROWS, seq_len=seq)
    time_major = lambda a: a.reshape(n_dec, t_new, d).transpose(1, 0, 2)
    u_s, conv_state_s = _conv_sample(time_major(bg_s), time_major(z_s), state_conv[0].transpose(1, 0, 2), conv_w[0])
    hs = _mm_res_ln(u_s.transpose(1, 0, 2).reshape(n_dec * t_new, d), w_out, hs,
                    vec(ln_mix_g[1]), vec(ln_mix_b[1]), tm=hs.shape[0])
    hp, = peer(hp, 1, u1, v1)
    hs, = peer(hs, 1, u1, v1)

    new_k_sample = k_s[:, None]
    new_v_sample = v_s[:, None]
    new_conv_prompt = z_p.reshape(batch, seq, d)[:, seq - (CONV_WIDTH - 1):][None]
    new_conv_sample = conv_state_s.transpose(1, 0, 2)[None]
    return (hp.reshape(batch, seq, d), hs.reshape(n_dec, t_new, d), new_k_prompt, new_v_prompt,
            new_k_sample, new_v_sample, new_conv_prompt, new_conv_sample)
```

```python
import functools

import jax
import jax.numpy as jnp
from jax import lax
from jax.experimental import pallas as pl
from jax.experimental.pallas import tpu as pltpu

F32 = jnp.float32
BF16 = jnp.bfloat16

D_MODEL = 2048
N_HEADS = 16
HEAD_DIM = 128
PAGE_SIZE = 128
MOBA_BLOCK = 256
MOBA_TOPK = 3
PAGES_PER_BLOCK = MOBA_BLOCK // PAGE_SIZE
CONV_WIDTH = 3
PEER_HEADS = 8
PEER_NKEYS = 128
PEER_TOPK = 16
PEER_HALF = 128
LN_EPS = 1e-5
DEPTH = 2
DEEPNORM_ALPHA = (2.0 * DEPTH) ** 0.25
ATTN_SCALE = HEAD_DIM ** -0.5
LOG2_E = 1.4426950408889634
NEG_INF = float("-inf")

V7X_LANES = 128
V7X_SUBLANES = 8
V7X_VMEM_LIMIT_BYTES = 60 * 1024 * 1024

_CONTRACT_LAST = (((1,), (1,)), ((), ()))
_CONTRACT_FIRST = (((0,), (0,)), ((), ()))
_BATCH_CONTRACT_LAST = (((2,), (2,)), ((0,), (0,)))
_BATCH_MATMUL = (((2,), (1,)), ((0,), (0,)))


def _params(n_axes):
    return pltpu.CompilerParams(dimension_semantics=("arbitrary",) * n_axes,
                                vmem_limit_bytes=V7X_VMEM_LIMIT_BYTES)


def _dot(a, b):
    return jnp.dot(a, b, preferred_element_type=F32)


def _dot_nt(a, b):
    return lax.dot_general(a, b, _CONTRACT_LAST, preferred_element_type=F32)


def _split_bf16(x):
    hi = x.astype(BF16)
    lo = (x - hi.astype(F32)).astype(BF16)
    return hi, lo


def _layer_norm(x, g, b):
    mu = jnp.mean(x, axis=-1, keepdims=True)
    xc = x - mu
    var = jnp.mean(xc * xc, axis=-1, keepdims=True)
    return xc * lax.rsqrt(var + LN_EPS) * g + b


def _mm_kernel(x_ref, w_ref, o_ref):
    o_ref[...] = _dot(x_ref[...].astype(BF16), w_ref[...])


def _mm(x, w, tm, tn):
    m, k = x.shape
    n = w.shape[1]
    return pl.pallas_call(
        _mm_kernel,
        grid=(n // tn, m // tm),
        in_specs=[pl.BlockSpec((tm, k), lambda j, i: (i, 0)),
                  pl.BlockSpec((k, tn), lambda j, i: (0, j))],
        out_specs=pl.BlockSpec((tm, tn), lambda j, i: (i, j)),
        out_shape=jax.ShapeDtypeStruct((m, n), F32),
        compiler_params=_params(2),
        name="proj",
    )(x, w)


def _mm_res_ln_kernel(x_ref, w_ref, h_ref, g_ref, b_ref, o_ref):
    y = _dot(x_ref[...].astype(BF16), w_ref[...])
    o_ref[...] = _layer_norm(DEEPNORM_ALPHA * h_ref[...] + y, g_ref[...], b_ref[...])


def _mm_res_ln(x, w, h, g, b, tm):
    m, k = x.shape
    d = w.shape[1]
    row = lambda i: (i, 0)
    fixed = lambda i: (0, 0)
    return pl.pallas_call(
        _mm_res_ln_kernel,
        grid=(m // tm,),
        in_specs=[pl.BlockSpec((tm, k), row), pl.BlockSpec((k, d), fixed),
                  pl.BlockSpec((tm, d), row), pl.BlockSpec((1, d), fixed), pl.BlockSpec((1, d), fixed)],
        out_specs=pl.BlockSpec((tm, d), row),
        out_shape=jax.ShapeDtypeStruct((m, d), F32),
        compiler_params=_params(1),
        name="out_proj_ln",
    )(x, w, h, g, b)


def _conv_in_kernel(x_ref, wb_ref, wc_ref, wh_ref, bg_ref, z_ref):
    xb = x_ref[...].astype(BF16)
    bg_ref[...] = _dot(xb, wb_ref[...])
    z_ref[...] = _dot(xb, wc_ref[...]) * _dot(xb, wh_ref[...])


def _conv_in(x, w_in, tm, tn):
    m, k = x.shape
    d = w_in.shape[1] // 3
    nb = d // tn
    xs = pl.BlockSpec((tm, k), lambda j, i: (i, 0))
    ws = [pl.BlockSpec((k, tn), functools.partial(lambda j, i, off: (0, j + off), off=part * nb))
          for part in range(3)]
    os_ = pl.BlockSpec((tm, tn), lambda j, i: (i, j))
    return pl.pallas_call(
        _conv_in_kernel,
        grid=(nb, m // tm),
        in_specs=[xs] + ws,
        out_specs=[os_, os_],
        out_shape=[jax.ShapeDtypeStruct((m, d), F32)] * 2,
        compiler_params=_params(2),
        name="conv_in",
    )(x, w_in, w_in, w_in)


def _conv_out_prompt_kernel(bg_ref, z_ref, zp_ref, cw_ref, w_ref, h_ref, g_ref, b_ref, o_ref, *, tiles_per_seq):
    i = pl.program_id(0)
    z = z_ref[...]
    seq_start = (i % tiles_per_seq) == 0
    zp = jnp.where(seq_start, 0.0, zp_ref[...])
    row = lax.broadcasted_iota(jnp.int32, z.shape, 0)
    last = V7X_SUBLANES - 1
    z1 = jnp.where(row == 0, zp[last:last + 1, :], pltpu.roll(z, 1, 0))
    z2 = jnp.where(row == 0, zp[last - 1:last, :],
                   jnp.where(row == 1, zp[last:last + 1, :], pltpu.roll(z, 2, 0)))
    cw = cw_ref[...]
    y = cw[0:1, :] * z2 + cw[1:2, :] * z1 + cw[2:3, :] * z
    u = (bg_ref[...] * y).astype(BF16)
    o_ref[...] = _layer_norm(DEEPNORM_ALPHA * h_ref[...] + _dot(u, w_ref[...]), g_ref[...], b_ref[...])


def _conv_out_prompt(bg, z, conv_w, w_out, h, g, b, tm, seq_len):
    m, d = z.shape
    row = lambda i: (i, 0)
    fixed = lambda i: (0, 0)
    halo = lambda i: (jnp.maximum(i * (tm // V7X_SUBLANES) - 1, 0), 0)
    return pl.pallas_call(
        functools.partial(_conv_out_prompt_kernel, tiles_per_seq=seq_len // tm),
        grid=(m // tm,),
        in_specs=[pl.BlockSpec((tm, d), row), pl.BlockSpec((tm, d), row),
                  pl.BlockSpec((V7X_SUBLANES, d), halo), pl.BlockSpec((CONV_WIDTH, d), fixed),
                  pl.BlockSpec((d, d), fixed), pl.BlockSpec((tm, d), row),
                  pl.BlockSpec((1, d), fixed), pl.BlockSpec((1, d), fixed)],
        out_specs=pl.BlockSpec((tm, d), row),
        out_shape=jax.ShapeDtypeStruct((m, d), F32),
        compiler_params=_params(1),
        name="conv_out_prompt",
    )(bg, z, z, conv_w, w_out, h, g, b)


def _conv_sample_kernel(bg_ref, z_ref, st_ref, cw_ref, u_ref, ns_ref, *, t):
    cw = cw_ref[...]
    zp = [st_ref[j] for j in range(CONV_WIDTH - 1)] + [z_ref[j] for j in range(t)]
    for j in range(t):
        y = cw[0:1, :] * zp[j] + cw[1:2, :] * zp[j + 1] + cw[2:3, :] * zp[j + 2]
        u_ref[j] = bg_ref[j] * y
    for j in range(CONV_WIDTH - 1):
        ns_ref[j] = zp[t + j]


def _conv_sample(bg, z, state, conv_w):
    t, n, d = z.shape
    return pl.pallas_call(
        functools.partial(_conv_sample_kernel, t=t),
        out_shape=[jax.ShapeDtypeStruct((t, n, d), F32), jax.ShapeDtypeStruct((CONV_WIDTH - 1, n, d), F32)],
        name="conv_sample",
    )(bg, z, state, conv_w)


def _topk_mask(g, n_valid, lane, n_cand):
    rank = jnp.zeros(g.shape, F32)
    for c in range(n_cand):
        col = g[..., c:c + 1]
        beats = (col > g) | ((col == g) & (c < lane))
        rank = rank + jnp.where(beats & (c < n_valid), 1.0, 0.0)
    return (lane < n_valid) & (rank < MOBA_TOPK)


def _side_cast_specs(tables, layer, n_steps, step_index):
    in_specs, out_specs, out_shapes = [], [], []
    for t in tables:
        _, rows, cols = t.shape
        slab = rows // n_steps
        assert slab * n_steps == rows and slab % (2 * V7X_SUBLANES) == 0
        in_specs.append(pl.BlockSpec((None, slab, cols), lambda *g: (layer, step_index(*g), 0)))
        out_specs.append(pl.BlockSpec((slab, cols), lambda *g: (step_index(*g), 0)))
        out_shapes.append(jax.ShapeDtypeStruct((rows, cols), BF16))
    return in_specs, out_specs, out_shapes


def _side_cast(src_refs, dst_refs):
    for src, dst in zip(src_refs, dst_refs):
        dst[...] = src[...].astype(dst.dtype)


def _moba_prompt_kernel(q_ref, k_ref, v_ref, *refs, n_blk, n_side):
    side_in, (o_ref, kp_ref, vp_ref) = refs[:n_side], refs[n_side:n_side + 3]
    side_out, km_ref = refs[n_side + 3:2 * n_side + 3], refs[2 * n_side + 3]
    _side_cast(side_in, side_out)
    blk = MOBA_BLOCK
    k = k_ref[...]
    v = v_ref[...]
    kp_ref[...] = k.reshape(kp_ref.shape)
    vp_ref[...] = v.reshape(vp_ref.shape)
    k_bf = k.astype(BF16)
    vt_bf = v.T.astype(BF16)
    km_ref[...] = jnp.zeros(km_ref.shape, F32)
    for n in range(n_blk):
        km_ref[n:n + 1, :] = jnp.mean(k[n * blk:(n + 1) * blk, :], axis=0, keepdims=True)
    q_hi, q_lo = _split_bf16(q_ref[...])
    km_hi, km_lo = _split_bf16(km_ref[...])
    gate_t = _dot_nt(km_hi, q_hi) + _dot_nt(km_hi, q_lo) + _dot_nt(km_lo, q_hi)
    blk_row = lax.broadcasted_iota(jnp.int32, (km_ref.shape[0], blk), 0)
    key_id = lax.broadcasted_iota(jnp.int32, (blk, blk), 0)
    qry_id = lax.broadcasted_iota(jnp.int32, (blk, blk), 1)

    def selected(qb):
        g = gate_t[:, qb * blk:(qb + 1) * blk]
        rank = jnp.zeros(g.shape, F32)
        for c in range(qb):
            gc = g[c:c + 1, :]
            beats = (gc > g) | ((gc == g) & (c < blk_row))
            rank = rank + jnp.where(beats, 1.0, 0.0)
        return jnp.where((blk_row < qb) & (rank < MOBA_TOPK), 1.0, 0.0)

    pair = 2
    for qp in range(n_blk // pair):
        qbs = [pair * qp + a for a in range(pair)]
        qs = slice(qbs[0] * blk, (qbs[-1] + 1) * blk)
        n_keys = (qbs[-1] + 1) * blk
        s = _dot_nt(k_bf[:n_keys], q_hi[qs]) * (ATTN_SCALE * LOG2_E)
        sel = [selected(qb) if qb > MOBA_TOPK else None for qb in qbs]
        pieces = []
        for n in range(qbs[-1] + 1):
            halves = []
            for a, qb in enumerate(qbs):
                sn = s[n * blk:(n + 1) * blk, a * blk:(a + 1) * blk]
                if n == qb:
                    sn = jnp.where(key_id <= qry_id, sn, NEG_INF)
                elif n > qb:
                    sn = jnp.full(sn.shape, NEG_INF, F32)
                elif sel[a] is not None:
                    sn = jnp.where(sel[a][n:n + 1, :] > 0.5, sn, NEG_INF)
                halves.append(sn)
            pieces.append(jnp.concatenate(halves, axis=1))
        m = jnp.max(pieces[0], axis=0, keepdims=True)
        for sn in pieces[1:]:
            m = jnp.maximum(m, jnp.max(sn, axis=0, keepdims=True))
        p = [jnp.exp2(sn - m) for sn in pieces]
        l = jnp.sum(p[0], axis=0, keepdims=True)
        for pn in p[1:]:
            l = l + jnp.sum(pn, axis=0, keepdims=True)
        p_bf = jnp.concatenate([pn.astype(BF16) for pn in p], axis=0)
        out_t = _dot(vt_bf[:, :n_keys], p_bf) * (1.0 / l)
        o_ref[qs, :] = out_t.T.astype(o_ref.dtype)


def _moba_prompt(qkv, batch, seq, cast_tables=(), cast_layer=0):
    n_blk = seq // MOBA_BLOCK
    assert n_blk * MOBA_BLOCK == seq and n_blk % 2 == 0
    n_pages = seq // PAGE_SIZE
    hd = HEAD_DIM
    km_rows = -(-n_blk // V7X_SUBLANES) * V7X_SUBLANES
    col = lambda part: pl.BlockSpec((seq, hd), lambda b, h: (b, part * N_HEADS + h))
    page_spec = pl.BlockSpec((None, n_pages, None, None, PAGE_SIZE, hd), lambda b, h: (b, 0, 0, h, 0, 0))
    page_shape = jax.ShapeDtypeStruct((batch, n_pages, 1, N_HEADS, PAGE_SIZE, hd), F32)
    side_in, side_out, side_shapes = _side_cast_specs(cast_tables, cast_layer, batch * N_HEADS,
                                                      lambda b, h: b * N_HEADS + h)
    return pl.pallas_call(
        functools.partial(_moba_prompt_kernel, n_blk=n_blk, n_side=len(cast_tables)),
        grid=(batch, N_HEADS),
        in_specs=[col(0), col(1), col(2)] + side_in,
        out_specs=[pl.BlockSpec((seq, hd), lambda b, h: (b, h)), page_spec, page_spec] + side_out,
        out_shape=[jax.ShapeDtypeStruct((batch * seq, D_MODEL), BF16), page_shape, page_shape] + side_shapes,
        scratch_shapes=[pltpu.VMEM((km_rows, hd), F32)],
        compiler_params=_params(2),
        name="moba_prompt",
    )(qkv, qkv, qkv, *cast_tables)


_BATCH_QK = functools.partial(lax.dot_general, dimension_numbers=_BATCH_CONTRACT_LAST, preferred_element_type=F32)
_BATCH_PV = functools.partial(lax.dot_general, dimension_numbers=_BATCH_MATMUL, preferred_element_type=F32)

_DEC_PAGES_PER_STEP = 8 * PAGES_PER_BLOCK


def _page_specs(n_heads, layer):
    def spec(which):
        return pl.BlockSpec((None, None, n_heads, PAGE_SIZE, HEAD_DIM),
                            lambda i, j, pt: (pt[i, _DEC_PAGES_PER_STEP * j + which], layer, 0, 0, 0))
    return [spec(w) for w in range(_DEC_PAGES_PER_STEP)]


def _dec_k_pass_kernel(pt_ref, q_ref, *refs):
    del pt_ref
    k_refs = refs[:_DEC_PAGES_PER_STEP]
    km_ref, s_ref = refs[_DEC_PAGES_PER_STEP:]
    q_bf = q_ref[...].astype(BF16)
    sums = []
    for pg, k_ref in enumerate(k_refs):
        k = k_ref[...]
        sums.append(jnp.sum(k, axis=1))
        s_ref[:, :, pg * PAGE_SIZE:(pg + 1) * PAGE_SIZE] = _BATCH_QK(q_bf, k.astype(BF16)) * ATTN_SCALE
    for b in range(_DEC_PAGES_PER_STEP // PAGES_PER_BLOCK):
        tot = sums[b * PAGES_PER_BLOCK]
        for pg in range(1, PAGES_PER_BLOCK):
            tot = tot + sums[b * PAGES_PER_BLOCK + pg]
        km_ref[b] = tot / MOBA_BLOCK


def _dec_k_pass(q, cache_k, page_table, layer):
    n, h, t8, hd = q.shape
    n_pages = page_table.shape[1]
    n_blocks = n_pages // PAGES_PER_BLOCK
    blocks_per_step = _DEC_PAGES_PER_STEP // PAGES_PER_BLOCK
    keys_per_step = _DEC_PAGES_PER_STEP * PAGE_SIZE
    return pl.pallas_call(
        _dec_k_pass_kernel,
        grid_spec=pltpu.PrefetchScalarGridSpec(
            num_scalar_prefetch=1, grid=(n, n_pages // _DEC_PAGES_PER_STEP),
            in_specs=[pl.BlockSpec((None, h, t8, hd), lambda i, j, pt: (i, 0, 0, 0))] + _page_specs(h, layer),
            out_specs=[pl.BlockSpec((None, blocks_per_step, h, hd), lambda i, j, pt: (i, j, 0, 0)),
                       pl.BlockSpec((None, h, t8, keys_per_step), lambda i, j, pt: (i, 0, 0, j))]),
        out_shape=[jax.ShapeDtypeStruct((n, n_blocks, h, hd), F32),
                   jax.ShapeDtypeStruct((n, h, t8, n_pages * PAGE_SIZE), F32)],
        compiler_params=_params(2),
        name="dec_k_pass",
    )(page_table, q, *([cache_k] * _DEC_PAGES_PER_STEP))


def _dec_softmax_kernel(q_ref, km_ref, s_ref, kn_ref, p_ref, pown_ref, *, n_blocks, t_new):
    q_hi, q_lo = _split_bf16(q_ref[...])
    km_hi, km_lo = _split_bf16(km_ref[...])
    gate = _BATCH_QK(q_hi, km_hi) + _BATCH_QK(q_lo, km_hi) + _BATCH_QK(q_hi, km_lo)
    lane = lax.broadcasted_iota(jnp.int32, gate.shape, 2)
    sel = jnp.where(_topk_mask(gate, n_blocks, lane, n_blocks), 1.0, 0.0)

    t_id = lax.broadcasted_iota(jnp.int32, (gate.shape[0], gate.shape[1], 1), 1)
    qf = q_hi.astype(F32)
    kf = kn_ref[...].astype(BF16).astype(F32)
    s_own = [jnp.where(c <= t_id, jnp.sum(qf * kf[:, c:c + 1, :], axis=-1, keepdims=True) * ATTN_SCALE, NEG_INF)
             for c in range(t_new)]
    m = s_own[0]
    for c in range(1, t_new):
        m = jnp.maximum(m, s_own[c])
    pieces = []
    for b in range(n_blocks):
        sb = jnp.where(sel[:, :, b:b + 1] > 0.5, s_ref[:, :, b * MOBA_BLOCK:(b + 1) * MOBA_BLOCK], NEG_INF)
        pieces.append(sb)
        m = jnp.maximum(m, jnp.max(sb, axis=-1, keepdims=True))
    e_own = [jnp.exp(s - m) for s in s_own]
    l = e_own[0]
    for e in e_own[1:]:
        l = l + e
    e_past = [jnp.exp(sb - m) for sb in pieces]
    for e in e_past:
        l = l + jnp.sum(e, axis=-1, keepdims=True)
    for b, e in enumerate(e_past):
        p_ref[:, :, b * MOBA_BLOCK:(b + 1) * MOBA_BLOCK] = e / l
    own = jnp.zeros(pown_ref.shape, F32)
    own_lane = lax.broadcasted_iota(jnp.int32, pown_ref.shape, 2)
    for c in range(t_new):
        own = jnp.where(own_lane == c, e_own[c] / l, own)
    pown_ref[...] = own


def _dec_softmax(q, k_mean, scores, k_new, n_blocks, t_new):
    n, h, t8, hd = q.shape
    past = scores.shape[-1]
    spec = lambda rows, cols: pl.BlockSpec((None, h, rows, cols), lambda i: (i, 0, 0, 0))
    return pl.pallas_call(
        functools.partial(_dec_softmax_kernel, n_blocks=n_blocks, t_new=t_new),
        grid=(n,),
        in_specs=[spec(t8, hd), spec(V7X_LANES, hd), spec(t8, past), spec(t8, hd)],
        out_specs=[spec(t8, past), spec(t8, V7X_LANES)],
        out_shape=[jax.ShapeDtypeStruct((n, h, t8, past), F32), jax.ShapeDtypeStruct((n, h, t8, V7X_LANES), F32)],
        compiler_params=_params(1),
        name="dec_softmax",
    )(q, k_mean, scores, k_new)


def _dec_v_pass_kernel(pt_ref, p_ref, pown_ref, vn_ref, *refs, t_new):
    del pt_ref
    v_refs = refs[:_DEC_PAGES_PER_STEP]
    o_ref = refs[_DEC_PAGES_PER_STEP]

    @pl.when(pl.program_id(1) == 0)
    def _():
        p_own = pown_ref[...].astype(BF16).astype(F32)
        vf = vn_ref[...].astype(BF16).astype(F32)
        acc = jnp.zeros(o_ref.shape, F32)
        for c in range(t_new):
            acc = acc + p_own[:, :, c:c + 1] * vf[:, c:c + 1, :]
        o_ref[...] = acc

    tot = o_ref[...]
    for pg, v_ref in enumerate(v_refs):
        p = p_ref[:, :, pg * PAGE_SIZE:(pg + 1) * PAGE_SIZE].astype(BF16)
        tot = tot + _BATCH_PV(p, v_ref[...].astype(BF16))
    o_ref[...] = tot


def _dec_v_pass(probs, p_own, v_new, cache_v, page_table, layer, t_new):
    n, h, t8, hd = v_new.shape
    n_pages = page_table.shape[1]
    keys_per_step = _DEC_PAGES_PER_STEP * PAGE_SIZE
    tok = lambda cols: pl.BlockSpec((None, h, t8, cols), lambda i, j, pt: (i, 0, 0, 0))
    return pl.pallas_call(
        functools.partial(_dec_v_pass_kernel, t_new=t_new),
        grid_spec=pltpu.PrefetchScalarGridSpec(
            num_scalar_prefetch=1, grid=(n, n_pages // _DEC_PAGES_PER_STEP),
            in_specs=[pl.BlockSpec((None, h, t8, keys_per_step), lambda i, j, pt: (i, 0, 0, j)),
                      tok(V7X_LANES), tok(hd)] + _page_specs(h, layer),
            out_specs=tok(hd)),
        out_shape=jax.ShapeDtypeStruct((n, h, t8, hd), F32),
        compiler_params=_params(2),
        name="dec_v_pass",
    )(page_table, probs, p_own, v_new, *([cache_v] * _DEC_PAGES_PER_STEP))


def _top_values(s, k):
    vals = []
    for r in range(k):
        m = jnp.max(s, axis=0, keepdims=True)
        vals.append(m)
        if r + 1 < k:
            s = jnp.where(s == m, NEG_INF, s)
    return vals


def _batcher_sort_network(lo, hi):
    def merge(lo, hi, r):
        step = 2 * r
        if step < hi - lo:
            yield from merge(lo, hi, step)
            yield from merge(lo + r, hi, step)
            yield from ((i, i + r) for i in range(lo + r, hi - r, step))
        else:
            yield (lo, lo + r)
    if hi - lo >= 1:
        mid = lo + (hi - lo) // 2
        yield from _batcher_sort_network(lo, mid)
        yield from _batcher_sort_network(mid + 1, hi)
        yield from merge(lo, hi, 1)


def _compare_exchange(rows, i, j):
    rows[i], rows[j] = jnp.maximum(rows[i], rows[j]), jnp.minimum(rows[i], rows[j])


def _top_values_network(s):
    sub = V7X_SUBLANES
    n = s.shape[0] // sub
    assert n * sub == s.shape[0] and n & (n - 1) == 0
    slabs = [s[v * sub:(v + 1) * sub, :] for v in range(n)]
    for i, j in _batcher_sort_network(0, n - 1):
        _compare_exchange(slabs, i, j)
    shift = sub // 2
    while shift >= 1:
        partner = [pltpu.roll(x, shift, 0) for x in slabs]
        slabs = [jnp.maximum(slabs[v], partner[n - 1 - v]) for v in range(n)]
        d = n // 2
        while d >= 1:
            for v in range(n):
                if v & d == 0:
                    _compare_exchange(slabs, v, v + d)
            d //= 2
        shift //= 2
    return [x[0:1, :] for x in slabs]


_PEER_CANDIDATES = [(a, b) for a in range(PEER_TOPK) for b in range(PEER_TOPK // (a + 1))]
_PEER_CAND_ROWS = -(-len(_PEER_CANDIDATES) // V7X_SUBLANES) * V7X_SUBLANES


def _peer_route_kernel(x_ref, wq_ref, sk_ref, s2_ref, e2_ref, tau_ref, e1_ref, cand_ref):
    q = _dot(x_ref[...].astype(BF16), wq_ref[...]).astype(BF16)
    cand_ref[...] = jnp.full(cand_ref.shape, NEG_INF, F32)
    for h in range(PEER_HEADS):
        s = []
        for p in range(2):
            c0 = (2 * h + p) * PEER_HALF
            s.append(_dot_nt(sk_ref[h, p], q[:, c0:c0 + PEER_HALF]))
        v1 = _top_values_network(s[0])
        v2 = _top_values_network(s[1])
        assert len(v1) == len(v2) == PEER_TOPK
        for r, (a, b) in enumerate(_PEER_CANDIDATES):
            cand_ref[r:r + 1, :] = v1[a] + v2[b]
        cand = cand_ref[...]
        thr = _top_values(cand, PEER_TOPK)[-1]
        z = jnp.sum(jnp.where(cand >= thr, jnp.exp(cand - (v1[0] + v2[0])), 0.0), axis=0, keepdims=True)
        s2_ref[h] = s[1]
        e2_ref[h] = jnp.exp(s[1] - v2[0])
        tau_ref[h] = thr - s[0]
        e1_ref[h] = 0.5 * jnp.exp(s[0] - v1[0]) / z


def _peer_route(x, w_q, sub_keys, tm):
    m, d = x.shape
    out_spec = pl.BlockSpec((PEER_HEADS, PEER_NKEYS, tm), lambda i: (0, 0, i))
    out_shape = jax.ShapeDtypeStruct((PEER_HEADS, PEER_NKEYS, m), F32)
    return pl.pallas_call(
        _peer_route_kernel,
        grid=(m // tm,),
        in_specs=[pl.BlockSpec((tm, d), lambda i: (i, 0)),
                  pl.BlockSpec(w_q.shape, lambda i: (0, 0)),
                  pl.BlockSpec(sub_keys.shape, lambda i: (0, 0, 0, 0))],
        out_specs=[out_spec] * 4,
        out_shape=[out_shape] * 4,
        scratch_shapes=[pltpu.VMEM((_PEER_CAND_ROWS, tm), F32)],
        compiler_params=_params(1),
        name="peer_route",
    )(x, w_q, sub_keys)


def _gelu_x2(x):
    return x * (1.0 + lax.erf(x * (2.0 ** -0.5)))


def _peer_expert_kernel(x_ref, u_ref, v_ref, s2_ref, e2_ref, tau_ref, e1_ref, g_ref, b_ref,
                        *refs, te, tm, n_side):
    side_in, o_ref, side_out = refs[:n_side], refs[n_side], refs[n_side + 1:2 * n_side + 1]
    xb_ref, st_ref, aw_ref = refs[2 * n_side + 1:]
    _side_cast(side_in, side_out)
    c = pl.program_id(1)

    @pl.when(c == 0)
    def _():
        o_ref[...] = jnp.zeros(o_ref.shape, F32)
        xb_ref[...] = x_ref[...].astype(BF16)

    st_ref[...] = _dot_nt(u_ref[...], xb_ref[...])

    rows_per_chunk = te // PEER_NKEYS
    chunks_per_group = max(V7X_SUBLANES // rows_per_chunk, 1)
    group = pl.multiple_of((c * rows_per_chunk // V7X_SUBLANES) * V7X_SUBLANES, V7X_SUBLANES)
    part = c % chunks_per_group

    def row_of(ref, h, r, cols):
        base = group + (r // V7X_SUBLANES) * V7X_SUBLANES
        x8 = ref[h, pl.ds(base, V7X_SUBLANES), cols]
        r8 = r % V7X_SUBLANES
        row = x8[r8:r8 + 1, :]
        for s in range(1, chunks_per_group):
            o = s * rows_per_chunk + r8
            row = jnp.where(part == s, x8[o:o + 1, :], row)
        return row

    for r in range(rows_per_chunk):
        rows = slice(r * PEER_NKEYS, (r + 1) * PEER_NKEYS)
        for tb in range(tm // V7X_LANES):
            cols = slice(tb * V7X_LANES, (tb + 1) * V7X_LANES)
            w = None
            for h in range(PEER_HEADS):
                tau = row_of(tau_ref, h, r, cols)
                e1 = row_of(e1_ref, h, r, cols)
                wh = jnp.where(s2_ref[h, :, cols] >= tau, e2_ref[h, :, cols] * e1, 0.0)
                w = wh if w is None else w + wh
            aw_ref[rows, cols] = (_gelu_x2(st_ref[rows, cols]) * w).astype(BF16)
    o_ref[...] += lax.dot_general(aw_ref[...], v_ref[...], _CONTRACT_FIRST, preferred_element_type=F32)

    @pl.when(c == pl.num_programs(1) - 1)
    def _():
        o_ref[...] = _layer_norm(DEEPNORM_ALPHA * x_ref[...] + o_ref[...], g_ref[...], b_ref[...])


def _peer_experts(x, u, v, route, g, b, tm, te, cast_tables=(), cast_layer=0):
    m, d = x.shape
    n_exp = u.shape[0]
    n_chunks = n_exp // te
    tok = pl.BlockSpec((tm, d), lambda i, c: (i, 0))
    tab = pl.BlockSpec((te, d), lambda i, c: (c, 0))
    rt = pl.BlockSpec((PEER_HEADS, PEER_NKEYS, tm), lambda i, c: (0, 0, i))
    vec = pl.BlockSpec((1, d), lambda i, c: (0, 0))
    side_in, side_out, side_shapes = _side_cast_specs(cast_tables, cast_layer, (m // tm) * n_chunks,
                                                      lambda i, c: i * n_chunks + c)
    return pl.pallas_call(
        functools.partial(_peer_expert_kernel, te=te, tm=tm, n_side=len(cast_tables)),
        grid=(m // tm, n_chunks),
        in_specs=[tok, tab, tab, rt, rt, rt, rt, vec, vec] + side_in,
        out_specs=[tok] + side_out,
        out_shape=[jax.ShapeDtypeStruct((m, d), F32)] + side_shapes,
        scratch_shapes=[pltpu.VMEM((tm, d), BF16), pltpu.VMEM((te, tm), F32), pltpu.VMEM((te, tm), BF16)],
        compiler_params=_params(2),
        name="peer_experts",
    )(x, u, v, *route, g, b, *cast_tables)


def _peer_layer(x, w_q, sub_keys, u, v, g, b, tm, te, cast_tables=(), cast_layer=0):
    route = _peer_route(x, w_q, sub_keys, tm)
    return _peer_experts(x, u, v, route, g, b, tm, te, cast_tables, cast_layer)


_PROJ_TILE = (512, 1024)
_CONV_IN_TILE = (512, 512)
_LN_ROWS = 256
_PEER_TILE = (512, 1024)


def _row_tile(m, pref):
    return min(m, pref)


def kernel(x_prompt, x_sample, cache_k, cache_v, state_conv, page_table, attn_w_qkv, attn_w_o,
           conv_w_in, conv_w, conv_w_out, ln_mix_g, ln_mix_b, ln_ffn_g, ln_ffn_b,
           peer_w_q, peer_sub_keys, peer_u, peer_v):
    batch, seq, d = x_prompt.shape
    n_dec, t_new, _ = x_sample.shape
    n_past_pages = page_table.shape[1]
    n_past_blocks = n_past_pages // PAGES_PER_BLOCK
    t8 = V7X_SUBLANES

    hp = x_prompt.reshape(batch * seq, d)
    hs = x_sample.reshape(n_dec * t_new, d)
    vec = lambda a: a.reshape(1, d)
    bf = lambda a: a.astype(BF16)

    expert_tables = (peer_u, peer_v)

    def peer(h, layer, u, v, cast_layer=None):
        m = h.shape[0]
        return _peer_layer(h, bf(peer_w_q[layer]), bf(peer_sub_keys[layer]), u, v,
                           vec(ln_ffn_g[layer]), vec(ln_ffn_b[layer]),
                           tm=_row_tile(m, _PEER_TILE[0]), te=_PEER_TILE[1],
                           cast_tables=() if cast_layer is None else expert_tables, cast_layer=cast_layer or 0)

    w_qkv = bf(attn_w_qkv[0])
    w_o = bf(attn_w_o[0])
    qkv_p = _mm(hp, w_qkv, tm=_PROJ_TILE[0], tn=_PROJ_TILE[1])
    qkv_s = _mm(hs, w_qkv, tm=hs.shape[0], tn=_PROJ_TILE[1])
    o_p, new_k_prompt, new_v_prompt, u0, v0 = _moba_prompt(qkv_p, batch, seq, expert_tables, 0)

    heads = lambda a: a.reshape(n_dec, t_new, N_HEADS, HEAD_DIM).transpose(0, 2, 1, 3)
    q_s, k_s, v_s = [heads(a) for a in jnp.split(qkv_s, 3, axis=-1)]
    pad_t = lambda a: jnp.pad(a, ((0, 0), (0, 0), (0, t8 - t_new), (0, 0)))
    assert n_past_pages % _DEC_PAGES_PER_STEP == 0 and n_past_blocks <= V7X_LANES
    k_mean, scores = _dec_k_pass(pad_t(q_s), cache_k, page_table, 0)
    k_mean = jnp.pad(k_mean.transpose(0, 2, 1, 3), ((0, 0), (0, 0), (0, V7X_LANES - n_past_blocks), (0, 0)))
    probs, p_own = _dec_softmax(pad_t(q_s), k_mean, scores, pad_t(k_s), n_past_blocks, t_new)
    o_s = _dec_v_pass(probs, p_own, pad_t(v_s), cache_v, page_table, 0, t_new)
    o_s = o_s[:, :, :t_new].transpose(0, 2, 1, 3).reshape(n_dec * t_new, d)

    hp = _mm_res_ln(o_p, w_o, hp, vec(ln_mix_g[0]), vec(ln_mix_b[0]), tm=_LN_ROWS)
    hs = _mm_res_ln(o_s, w_o, hs, vec(ln_mix_g[0]), vec(ln_mix_b[0]), tm=hs.shape[0])
    hp, u1, v1 = peer(hp, 0, u0, v0, cast_layer=1)
    hs, = peer(hs, 0, u0, v0)

    w_in = bf(conv_w_in[0])
    w_out = bf(conv_w_out[0])
    bg_p, z_p = _conv_in(hp, w_in, tm=_CONV_IN_TILE[0], tn=_CONV_IN_TILE[1])
    bg_s, z_s = _conv_in(hs, w_in, tm=hs.shape[0], tn=_CONV_IN_TILE[1])
    hp = _conv_out_prompt(bg_p, z_p, conv_w[0], w_out, hp, vec(ln_mix_g[1]), vec(ln_mix_b[1]),
                          tm=_LN_ROWS, seq_len=seq)
    time_major = lambda a: a.reshape(n_dec, t_new, d).transpose(1, 0, 2)
    u_s, conv_state_s = _conv_sample(time_major(bg_s), time_major(z_s), state_conv[0].transpose(1, 0, 2), conv_w[0])
    hs = _mm_res_ln(u_s.transpose(1, 0, 2).reshape(n_dec * t_new, d), w_out, hs,
                    vec(ln_mix_g[1]), vec(ln_mix_b[1]), tm=hs.shape[0])
    hp, = peer(hp, 1, u1, v1)
    hs, = peer(hs, 1, u1, v1)

    new_k_sample = k_s[:, None]
    new_v_sample = v_s[:, None]
    new_conv_prompt = z_p.reshape(batch, seq, d)[:, seq - (CONV_WIDTH - 1):][None]
    new_conv_sample = conv_state_s.transpose(1, 0, 2)[None]
    return (hp.reshape(batch, seq, d), hs.reshape(n_dec, t_new, d), new_k_prompt, new_v_prompt,
            new_k_sample, new_v_sample, new_conv_prompt, new_conv_sample)
```

```python
import functools

import jax
import jax.numpy as jnp
from jax import lax
from jax.experimental import pallas as pl
from jax.experimental.pallas import tpu as pltpu

F32 = jnp.float32
BF16 = jnp.bfloat16

D_MODEL = 2048
N_HEADS = 16
HEAD_DIM = 128
PAGE_SIZE = 128
MOBA_BLOCK = 256
MOBA_TOPK = 3
PAGES_PER_BLOCK = MOBA_BLOCK // PAGE_SIZE
CONV_WIDTH = 3
PEER_HEADS = 8
PEER_NKEYS = 128
PEER_TOPK = 16
PEER_HALF = 128
LN_EPS = 1e-5
DEPTH = 2
DEEPNORM_ALPHA = (2.0 * DEPTH) ** 0.25
ATTN_SCALE = HEAD_DIM ** -0.5
LOG2_E = 1.4426950408889634
NEG_INF = float("-inf")

V7X_LANES = 128
V7X_SUBLANES = 8
V7X_VMEM_LIMIT_BYTES = 60 * 1024 * 1024

_CONTRACT_LAST = (((1,), (1,)), ((), ()))
_CONTRACT_FIRST = (((0,), (0,)), ((), ()))
_BATCH_CONTRACT_LAST = (((2,), (2,)), ((0,), (0,)))
_BATCH_MATMUL = (((2,), (1,)), ((0,), (0,)))


def _params(n_axes):
    return pltpu.CompilerParams(dimension_semantics=("arbitrary",) * n_axes,
                                vmem_limit_bytes=V7X_VMEM_LIMIT_BYTES)


def _dot(a, b):
    return jnp.dot(a, b, preferred_element_type=F32)


def _dot_nt(a, b):
    return lax.dot_general(a, b, _CONTRACT_LAST, preferred_element_type=F32)


def _split_bf16(x):
    hi = x.astype(BF16)
    lo = (x - hi.astype(F32)).astype(BF16)
    return hi, lo


def _layer_norm(x, g, b):
    mu = jnp.mean(x, axis=-1, keepdims=True)
    xc = x - mu
    var = jnp.mean(xc * xc, axis=-1, keepdims=True)
    return xc * lax.rsqrt(var + LN_EPS) * g + b


def _mm_kernel(x_ref, w_ref, o_ref):
    o_ref[...] = _dot(x_ref[...].astype(BF16), w_ref[...])


def _mm(x, w, tm, tn):
    m, k = x.shape
    n = w.shape[1]
    return pl.pallas_call(
        _mm_kernel,
        grid=(n // tn, m // tm),
        in_specs=[pl.BlockSpec((tm, k), lambda j, i: (i, 0)),
                  pl.BlockSpec((k, tn), lambda j, i: (0, j))],
        out_specs=pl.BlockSpec((tm, tn), lambda j, i: (i, j)),
        out_shape=jax.ShapeDtypeStruct((m, n), F32),
        compiler_params=_params(2),
        name="proj",
    )(x, w)


def _mm_res_ln_kernel(x_ref, w_ref, h_ref, g_ref, b_ref, o_ref):
    y = _dot(x_ref[...].astype(BF16), w_ref[...])
    o_ref[...] = _layer_norm(DEEPNORM_ALPHA * h_ref[...] + y, g_ref[...], b_ref[...])


def _mm_res_ln(x, w, h, g, b, tm):
    m, k = x.shape
    d = w.shape[1]
    row = lambda i: (i, 0)
    fixed = lambda i: (0, 0)
    return pl.pallas_call(
        _mm_res_ln_kernel,
        grid=(m // tm,),
        in_specs=[pl.BlockSpec((tm, k), row), pl.BlockSpec((k, d), fixed),
                  pl.BlockSpec((tm, d), row), pl.BlockSpec((1, d), fixed), pl.BlockSpec((1, d), fixed)],
        out_specs=pl.BlockSpec((tm, d), row),
        out_shape=jax.ShapeDtypeStruct((m, d), F32),
        compiler_params=_params(1),
        name="out_proj_ln",
    )(x, w, h, g, b)


def _conv_in_kernel(x_ref, wb_ref, wc_ref, wh_ref, bg_ref, z_ref):
    xb = x_ref[...].astype(BF16)
    bg_ref[...] = _dot(xb, wb_ref[...])
    z_ref[...] = _dot(xb, wc_ref[...]) * _dot(xb, wh_ref[...])


def _conv_in(x, w_in, tm, tn):
    m, k = x.shape
    d = w_in.shape[1] // 3
    nb = d // tn
    xs = pl.BlockSpec((tm, k), lambda j, i: (i, 0))
    ws = [pl.BlockSpec((k, tn), functools.partial(lambda j, i, off: (0, j + off), off=part * nb))
          for part in range(3)]
    os_ = pl.BlockSpec((tm, tn), lambda j, i: (i, j))
    return pl.pallas_call(
        _conv_in_kernel,
        grid=(nb, m // tm),
        in_specs=[xs] + ws,
        out_specs=[os_, os_],
        out_shape=[jax.ShapeDtypeStruct((m, d), F32)] * 2,
        compiler_params=_params(2),
        name="conv_in",
    )(x, w_in, w_in, w_in)


def _conv_out_prompt_kernel(bg_ref, z_ref, zp_ref, cw_ref, w_ref, h_ref, g_ref, b_ref, o_ref, *, tiles_per_seq):
    i = pl.program_id(0)
    z = z_ref[...]
    seq_start = (i % tiles_per_seq) == 0
    zp = jnp.where(seq_start, 0.0, zp_ref[...])
    row = lax.broadcasted_iota(jnp.int32, z.shape, 0)
    last = V7X_SUBLANES - 1
    z1 = jnp.where(row == 0, zp[last:last + 1, :], pltpu.roll(z, 1, 0))
    z2 = jnp.where(row == 0, zp[last - 1:last, :],
                   jnp.where(row == 1, zp[last:last + 1, :], pltpu.roll(z, 2, 0)))
    cw = cw_ref[...]
    y = cw[0:1, :] * z2 + cw[1:2, :] * z1 + cw[2:3, :] * z
    u = (bg_ref[...] * y).astype(BF16)
    o_ref[...] = _layer_norm(DEEPNORM_ALPHA * h_ref[...] + _dot(u, w_ref[...]), g_ref[...], b_ref[...])


def _conv_out_prompt(bg, z, conv_w, w_out, h, g, b, tm, seq_len):
    m, d = z.shape
    row = lambda i: (i, 0)
    fixed = lambda i: (0, 0)
    halo = lambda i: (jnp.maximum(i * (tm // V7X_SUBLANES) - 1, 0), 0)
    return pl.pallas_call(
        functools.partial(_conv_out_prompt_kernel, tiles_per_seq=seq_len // tm),
        grid=(m // tm,),
        in_specs=[pl.BlockSpec((tm, d), row), pl.BlockSpec((tm, d), row),
                  pl.BlockSpec((V7X_SUBLANES, d), halo), pl.BlockSpec((CONV_WIDTH, d), fixed),
                  pl.BlockSpec((d, d), fixed), pl.BlockSpec((tm, d), row),
                  pl.BlockSpec((1, d), fixed), pl.BlockSpec((1, d), fixed)],
        out_specs=pl.BlockSpec((tm, d), row),
        out_shape=jax.ShapeDtypeStruct((m, d), F32),
        compiler_params=_params(1),
        name="conv_out_prompt",
    )(bg, z, z, conv_w, w_out, h, g, b)


def _conv_sample_kernel(bg_ref, z_ref, st_ref, cw_ref, u_ref, ns_ref, *, t):
    cw = cw_ref[...]
    zp = [st_ref[j] for j in range(CONV_WIDTH - 1)] + [z_ref[j] for j in range(t)]
    for j in range(t):
        y = cw[0:1, :] * zp[j] + cw[1:2, :] * zp[j + 1] + cw[2:3, :] * zp[j + 2]
        u_ref[j] = bg_ref[j] * y
    for j in range(CONV_WIDTH - 1):
        ns_ref[j] = zp[t + j]


def _conv_sample(bg, z, state, conv_w):
    t, n, d = z.shape
    return pl.pallas_call(
        functools.partial(_conv_sample_kernel, t=t),
        out_shape=[jax.ShapeDtypeStruct((t, n, d), F32), jax.ShapeDtypeStruct((CONV_WIDTH - 1, n, d), F32)],
        name="conv_sample",
    )(bg, z, state, conv_w)


def _topk_mask(g, n_valid, lane, n_cand):
    rank = jnp.zeros(g.shape, F32)
    for c in range(n_cand):
        col = g[..., c:c + 1]
        beats = (col > g) | ((col == g) & (c < lane))
        rank = rank + jnp.where(beats & (c < n_valid), 1.0, 0.0)
    return (lane < n_valid) & (rank < MOBA_TOPK)


def _side_cast_specs(tables, layer, n_steps, step_index):
    in_specs, out_specs, out_shapes = [], [], []
    for t in tables:
        _, rows, cols = t.shape
        slab = rows // n_steps
        assert slab * n_steps == rows and slab % (2 * V7X_SUBLANES) == 0
        in_specs.append(pl.BlockSpec((None, slab, cols), lambda *g: (layer, step_index(*g), 0)))
        out_specs.append(pl.BlockSpec((slab, cols), lambda *g: (step_index(*g), 0)))
        out_shapes.append(jax.ShapeDtypeStruct((rows, cols), BF16))
    return in_specs, out_specs, out_shapes


def _side_cast(src_refs, dst_refs):
    for src, dst in zip(src_refs, dst_refs):
        dst[...] = src[...].astype(dst.dtype)


def _moba_prompt_kernel(q_ref, k_ref, v_ref, *refs, n_blk, n_side):
    side_in, (o_ref, kp_ref, vp_ref) = refs[:n_side], refs[n_side:n_side + 3]
    side_out, km_ref = refs[n_side + 3:2 * n_side + 3], refs[2 * n_side + 3]
    _side_cast(side_in, side_out)
    blk = MOBA_BLOCK
    k = k_ref[...]
    v = v_ref[...]
    kp_ref[...] = k.reshape(kp_ref.shape)
    vp_ref[...] = v.reshape(vp_ref.shape)
    k_bf = k.astype(BF16)
    vt_bf = v.T.astype(BF16)
    km_ref[...] = jnp.zeros(km_ref.shape, F32)
    for n in range(n_blk):
        km_ref[n:n + 1, :] = jnp.mean(k[n * blk:(n + 1) * blk, :], axis=0, keepdims=True)
    q_hi, q_lo = _split_bf16(q_ref[...])
    km_hi, km_lo = _split_bf16(km_ref[...])
    gate_t = _dot_nt(km_hi, q_hi) + _dot_nt(km_hi, q_lo) + _dot_nt(km_lo, q_hi)
    blk_row = lax.broadcasted_iota(jnp.int32, (km_ref.shape[0], blk), 0)
    key_id = lax.broadcasted_iota(jnp.int32, (blk, blk), 0)
    qry_id = lax.broadcasted_iota(jnp.int32, (blk, blk), 1)

    def selected(qb):
        g = gate_t[:, qb * blk:(qb + 1) * blk]
        rank = jnp.zeros(g.shape, F32)
        for c in range(qb):
            gc = g[c:c + 1, :]
            beats = (gc > g) | ((gc == g) & (c < blk_row))
            rank = rank + jnp.where(beats, 1.0, 0.0)
        return jnp.where((blk_row < qb) & (rank < MOBA_TOPK), 1.0, 0.0)

    pair = 2
    for qp in range(n_blk // pair):
        qbs = [pair * qp + a for a in range(pair)]
        qs = slice(qbs[0] * blk, (qbs[-1] + 1) * blk)
        n_keys = (qbs[-1] + 1) * blk
        s = _dot_nt(k_bf[:n_keys], q_hi[qs]) * (ATTN_SCALE * LOG2_E)
        sel = [selected(qb) if qb > MOBA_TOPK else None for qb in qbs]
        pieces = []
        for n in range(qbs[-1] + 1):
            halves = []
            for a, qb in enumerate(qbs):
                sn = s[n * blk:(n + 1) * blk, a * blk:(a + 1) * blk]
                if n == qb:
                    sn = jnp.where(key_id <= qry_id, sn, NEG_INF)
                elif n > qb:
                    sn = jnp.full(sn.shape, NEG_INF, F32)
                elif sel[a] is not None:
                    sn = jnp.where(sel[a][n:n + 1, :] > 0.5, sn, NEG_INF)
                halves.append(sn)
            pieces.append(jnp.concatenate(halves, axis=1))
        m = jnp.max(pieces[0], axis=0, keepdims=True)
        for sn in pieces[1:]:
            m = jnp.maximum(m, jnp.max(sn, axis=0, keepdims=True))
        p = [jnp.exp2(sn - m) for sn in pieces]
        l = jnp.sum(p[0], axis=0, keepdims=True)
        for pn in p[1:]:
            l = l + jnp.sum(pn, axis=0, keepdims=True)
        p_bf = jnp.concatenate([pn.astype(BF16) for pn in p], axis=0)
        out_t = _dot(vt_bf[:, :n_keys], p_bf) * (1.0 / l)
        o_ref[qs, :] = out_t.T.astype(o_ref.dtype)


def _moba_prompt(qkv, batch, seq, cast_tables=(), cast_layer=0):
    n_blk = seq // MOBA_BLOCK
    assert n_blk * MOBA_BLOCK == seq and n_blk % 2 == 0
    n_pages = seq // PAGE_SIZE
    hd = HEAD_DIM
    km_rows = -(-n_blk // V7X_SUBLANES) * V7X_SUBLANES
    col = lambda part: pl.BlockSpec((seq, hd), lambda b, h: (b, part * N_HEADS + h))
    page_spec = pl.BlockSpec((None, n_pages, None, None, PAGE_SIZE, hd), lambda b, h: (b, 0, 0, h, 0, 0))
    page_shape = jax.ShapeDtypeStruct((batch, n_pages, 1, N_HEADS, PAGE_SIZE, hd), F32)
    side_in, side_out, side_shapes = _side_cast_specs(cast_tables, cast_layer, batch * N_HEADS,
                                                      lambda b, h: b * N_HEADS + h)
    return pl.pallas_call(
        functools.partial(_moba_prompt_kernel, n_blk=n_blk, n_side=len(cast_tables)),
        grid=(batch, N_HEADS),
        in_specs=[col(0), col(1), col(2)] + side_in,
        out_specs=[pl.BlockSpec((seq, hd), lambda b, h: (b, h)), page_spec, page_spec] + side_out,
        out_shape=[jax.ShapeDtypeStruct((batch * seq, D_MODEL), BF16), page_shape, page_shape] + side_shapes,
        scratch_shapes=[pltpu.VMEM((km_rows, hd), F32)],
        compiler_params=_params(2),
        name="moba_prompt",
    )(qkv, qkv, qkv, *cast_tables)


_BATCH_QK = functools.partial(lax.dot_general, dimension_numbers=_BATCH_CONTRACT_LAST, preferred_element_type=F32)
_BATCH_PV = functools.partial(lax.dot_general, dimension_numbers=_BATCH_MATMUL, preferred_element_type=F32)

_DEC_PAGES_PER_STEP = 8 * PAGES_PER_BLOCK


def _page_specs(n_heads, layer):
    def spec(which):
        return pl.BlockSpec((None, None, n_heads, PAGE_SIZE, HEAD_DIM),
                            lambda i, j, pt: (pt[i, _DEC_PAGES_PER_STEP * j + which], layer, 0, 0, 0))
    return [spec(w) for w in range(_DEC_PAGES_PER_STEP)]


def _dec_k_pass_kernel(pt_ref, q_ref, *refs):
    del pt_ref
    k_refs = refs[:_DEC_PAGES_PER_STEP]
    km_ref, s_ref = refs[_DEC_PAGES_PER_STEP:]
    q_bf = q_ref[...].astype(BF16)
    sums = []
    for pg, k_ref in enumerate(k_refs):
        k = k_ref[...]
        sums.append(jnp.sum(k, axis=1))
        s_ref[:, :, pg * PAGE_SIZE:(pg + 1) * PAGE_SIZE] = _BATCH_QK(q_bf, k.astype(BF16)) * ATTN_SCALE
    for b in range(_DEC_PAGES_PER_STEP // PAGES_PER_BLOCK):
        tot = sums[b * PAGES_PER_BLOCK]
        for pg in range(1, PAGES_PER_BLOCK):
            tot = tot + sums[b * PAGES_PER_BLOCK + pg]
        km_ref[b] = tot / MOBA_BLOCK


def _dec_k_pass(q, cache_k, page_table, layer):
    n, h, t8, hd = q.shape
    n_pages = page_table.shape[1]
    n_blocks = n_pages // PAGES_PER_BLOCK
    blocks_per_step = _DEC_PAGES_PER_STEP // PAGES_PER_BLOCK
    keys_per_step = _DEC_PAGES_PER_STEP * PAGE_SIZE
    return pl.pallas_call(
        _dec_k_pass_kernel,
        grid_spec=pltpu.PrefetchScalarGridSpec(
            num_scalar_prefetch=1, grid=(n, n_pages // _DEC_PAGES_PER_STEP),
            in_specs=[pl.BlockSpec((None, h, t8, hd), lambda i, j, pt: (i, 0, 0, 0))] + _page_specs(h, layer),
            out_specs=[pl.BlockSpec((None, blocks_per_step, h, hd), lambda i, j, pt: (i, j, 0, 0)),
                       pl.BlockSpec((None, h, t8, keys_per_step), lambda i, j, pt: (i, 0, 0, j))]),
        out_shape=[jax.ShapeDtypeStruct((n, n_blocks, h, hd), F32),
                   jax.ShapeDtypeStruct((n, h, t8, n_pages * PAGE_SIZE), F32)],
        compiler_params=_params(2),
        name="dec_k_pass",
    )(page_table, q, *([cache_k] * _DEC_PAGES_PER_STEP))


def _dec_softmax_kernel(q_ref, km_ref, s_ref, kn_ref, p_ref, pown_ref, *, n_blocks, t_new):
    q_hi, q_lo = _split_bf16(q_ref[...])
    km_hi, km_lo = _split_bf16(km_ref[...])
    gate = _BATCH_QK(q_hi, km_hi) + _BATCH_QK(q_lo, km_hi) + _BATCH_QK(q_hi, km_lo)
    lane = lax.broadcasted_iota(jnp.int32, gate.shape, 2)
    sel = jnp.where(_topk_mask(gate, n_blocks, lane, n_blocks), 1.0, 0.0)

    t_id = lax.broadcasted_iota(jnp.int32, (gate.shape[0], gate.shape[1], 1), 1)
    qf = q_hi.astype(F32)
    kf = kn_ref[...].astype(BF16).astype(F32)
    s_own = [jnp.where(c <= t_id, jnp.sum(qf * kf[:, c:c + 1, :], axis=-1, keepdims=True) * ATTN_SCALE, NEG_INF)
             for c in range(t_new)]
    m = s_own[0]
    for c in range(1, t_new):
        m = jnp.maximum(m, s_own[c])
    pieces = []
    for b in range(n_blocks):
        sb = jnp.where(sel[:, :, b:b + 1] > 0.5, s_ref[:, :, b * MOBA_BLOCK:(b + 1) * MOBA_BLOCK], NEG_INF)
        pieces.append(sb)
        m = jnp.maximum(m, jnp.max(sb, axis=-1, keepdims=True))
    e_own = [jnp.exp(s - m) for s in s_own]
    l = e_own[0]
    for e in e_own[1:]:
        l = l + e
    e_past = [jnp.exp(sb - m) for sb in pieces]
    for e in e_past:
        l = l + jnp.sum(e, axis=-1, keepdims=True)
    for b, e in enumerate(e_past):
        p_ref[:, :, b * MOBA_BLOCK:(b + 1) * MOBA_BLOCK] = e / l
    own = jnp.zeros(pown_ref.shape, F32)
    own_lane = lax.broadcasted_iota(jnp.int32, pown_ref.shape, 2)
    for c in range(t_new):
        own = jnp.where(own_lane == c, e_own[c] / l, own)
    pown_ref[...] = own


def _dec_softmax(q, k_mean, scores, k_new, n_blocks, t_new):
    n, h, t8, hd = q.shape
    past = scores.shape[-1]
    spec = lambda rows, cols: pl.BlockSpec((None, h, rows, cols), lambda i: (i, 0, 0, 0))
    return pl.pallas_call(
        functools.partial(_dec_softmax_kernel, n_blocks=n_blocks, t_new=t_new),
        grid=(n,),
        in_specs=[spec(t8, hd), spec(V7X_LANES, hd), spec(t8, past), spec(t8, hd)],
        out_specs=[spec(t8, past), spec(t8, V7X_LANES)],
        out_shape=[jax.ShapeDtypeStruct((n, h, t8, past), F32), jax.ShapeDtypeStruct((n, h, t8, V7X_LANES), F32)],
        compiler_params=_params(1),
        name="dec_softmax",
    )(q, k_mean, scores, k_new)


def _dec_v_pass_kernel(pt_ref, p_ref, pown_ref, vn_ref, *refs, t_new):
    del pt_ref
    v_refs = refs[:_DEC_PAGES_PER_STEP]
    o_ref = refs[_DEC_PAGES_PER_STEP]

    @pl.when(pl.program_id(1) == 0)
    def _():
        p_own = pown_ref[...].astype(BF16).astype(F32)
        vf = vn_ref[...].astype(BF16).astype(F32)
        acc = jnp.zeros(o_ref.shape, F32)
        for c in range(t_new):
            acc = acc + p_own[:, :, c:c + 1] * vf[:, c:c + 1, :]
        o_ref[...] = acc

    tot = o_ref[...]
    for pg, v_ref in enumerate(v_refs):
        p = p_ref[:, :, pg * PAGE_SIZE:(pg + 1) * PAGE_SIZE].astype(BF16)
        tot = tot + _BATCH_PV(p, v_ref[...].astype(BF16))
    o_ref[...] = tot


def _dec_v_pass(probs, p_own, v_new, cache_v, page_table, layer, t_new):
    n, h, t8, hd = v_new.shape
    n_pages = page_table.shape[1]
    keys_per_step = _DEC_PAGES_PER_STEP * PAGE_SIZE
    tok = lambda cols: pl.BlockSpec((None, h, t8, cols), lambda i, j, pt: (i, 0, 0, 0))
    return pl.pallas_call(
        functools.partial(_dec_v_pass_kernel, t_new=t_new),
        grid_spec=pltpu.PrefetchScalarGridSpec(
            num_scalar_prefetch=1, grid=(n, n_pages // _DEC_PAGES_PER_STEP),
            in_specs=[pl.BlockSpec((None, h, t8, keys_per_step), lambda i, j, pt: (i, 0, 0, j)),
                      tok(V7X_LANES), tok(hd)] + _page_specs(h, layer),
            out_specs=tok(hd)),
        out_shape=jax.ShapeDtypeStruct((n, h, t8, hd), F32),
        compiler_params=_params(2),
        name="dec_v_pass",
    )(page_table, probs, p_own, v_new, *([cache_v] * _DEC_PAGES_PER_STEP))


def _top_values(s, k):
    vals = []
    for r in range(k):
        m = jnp.max(s, axis=0, keepdims=True)
        vals.append(m)
        if r + 1 < k:
            s = jnp.where(s == m, NEG_INF, s)
    return vals


def _batcher_sort_network(lo, hi):
    def merge(lo, hi, r):
        step = 2 * r
        if step < hi - lo:
            yield from merge(lo, hi, step)
            yield from merge(lo + r, hi, step)
            yield from ((i, i + r) for i in range(lo + r, hi - r, step))
        else:
            yield (lo, lo + r)
    if hi - lo >= 1:
        mid = lo + (hi - lo) // 2
        yield from _batcher_sort_network(lo, mid)
        yield from _batcher_sort_network(mid + 1, hi)
        yield from merge(lo, hi, 1)


def _compare_exchange(rows, i, j):
    rows[i], rows[j] = jnp.maximum(rows[i], rows[j]), jnp.minimum(rows[i], rows[j])


def _top_values_network(s):
    sub = V7X_SUBLANES
    n = s.shape[0] // sub
    assert n * sub == s.shape[0] and n & (n - 1) == 0
    slabs = [s[v * sub:(v + 1) * sub, :] for v in range(n)]
    for i, j in _batcher_sort_network(0, n - 1):
        _compare_exchange(slabs, i, j)
    shift = sub // 2
    while shift >= 1:
        partner = [pltpu.roll(x, shift, 0) for x in slabs]
        slabs = [jnp.maximum(slabs[v], partner[n - 1 - v]) for v in range(n)]
        d = n // 2
        while d >= 1:
            for v in range(n):
                if v & d == 0:
                    _compare_exchange(slabs, v, v + d)
            d //= 2
        shift //= 2
    return [x[0:1, :] for x in slabs]


_PEER_CANDIDATES = [(a, b) for a in range(PEER_TOPK) for b in range(PEER_TOPK // (a + 1))]
_PEER_CAND_ROWS = -(-len(_PEER_CANDIDATES) // V7X_SUBLANES) * V7X_SUBLANES


def _peer_route_kernel(x_ref, wq_ref, sk_ref, s2_ref, e2_ref, tau_ref, e1_ref, cand_ref):
    q = _dot(x_ref[...].astype(BF16), wq_ref[...]).astype(BF16)
    cand_ref[...] = jnp.full(cand_ref.shape, NEG_INF, F32)
    for h in range(PEER_HEADS):
        s = []
        for p in range(2):
            c0 = (2 * h + p) * PEER_HALF
            s.append(_dot_nt(sk_ref[h, p], q[:, c0:c0 + PEER_HALF]))
        v1 = _top_values_network(s[0])
        v2 = _top_values_network(s[1])
        assert len(v1) == len(v2) == PEER_TOPK
        sums = [v1[a] + v2[b] for a, b in _PEER_CANDIDATES]
        for r, c in enumerate(sums):
            cand_ref[r:r + 1, :] = c
        cand = cand_ref[...]
        thr = _top_values(cand, PEER_TOPK)[-1]
        z = jnp.sum(jnp.where(cand >= thr, jnp.exp(cand - (v1[0] + v2[0])), 0.0), axis=0, keepdims=True)
        inf = jnp.full(thr.shape, jnp.inf, F32)
        low = [inf] * PEER_TOPK
        for (a, b), c in zip(_PEER_CANDIDATES, sums):
            low[a] = jnp.where(c >= thr, jnp.minimum(low[a], v2[b]), low[a])
        tau = jnp.full(s[0].shape, jnp.inf, F32)
        for a in reversed(range(PEER_TOPK)):
            tau = jnp.where(s[0] >= v1[a], low[a], tau)
        s2_ref[h] = s[1]
        e2_ref[h] = jnp.exp(s[1] - v2[0])
        tau_ref[h] = tau
        e1_ref[h] = 0.5 * jnp.exp(s[0] - v1[0]) / z


def _peer_route(x, w_q, sub_keys, tm):
    m, d = x.shape
    out_spec = pl.BlockSpec((PEER_HEADS, PEER_NKEYS, tm), lambda i: (0, 0, i))
    out_shape = jax.ShapeDtypeStruct((PEER_HEADS, PEER_NKEYS, m), F32)
    return pl.pallas_call(
        _peer_route_kernel,
        grid=(m // tm,),
        in_specs=[pl.BlockSpec((tm, d), lambda i: (i, 0)),
                  pl.BlockSpec(w_q.shape, lambda i: (0, 0)),
                  pl.BlockSpec(sub_keys.shape, lambda i: (0, 0, 0, 0))],
        out_specs=[out_spec] * 4,
        out_shape=[out_shape] * 4,
        scratch_shapes=[pltpu.VMEM((_PEER_CAND_ROWS, tm), F32)],
        compiler_params=_params(1),
        name="peer_route",
    )(x, w_q, sub_keys)


def _gelu_x2(x):
    return x * (1.0 + lax.erf(x * (2.0 ** -0.5)))


def _peer_expert_kernel(x_ref, u_ref, v_ref, s2_ref, e2_ref, tau_ref, e1_ref, g_ref, b_ref,
                        *refs, te, tm, n_side):
    side_in, o_ref, side_out = refs[:n_side], refs[n_side], refs[n_side + 1:2 * n_side + 1]
    xb_ref, st_ref, aw_ref = refs[2 * n_side + 1:]
    _side_cast(side_in, side_out)
    c = pl.program_id(1)

    @pl.when(c == 0)
    def _():
        o_ref[...] = jnp.zeros(o_ref.shape, F32)
        xb_ref[...] = x_ref[...].astype(BF16)

    st_ref[...] = _dot_nt(u_ref[...], xb_ref[...])

    rows_per_chunk = te // PEER_NKEYS
    chunks_per_group = max(V7X_SUBLANES // rows_per_chunk, 1)
    group = pl.multiple_of((c * rows_per_chunk // V7X_SUBLANES) * V7X_SUBLANES, V7X_SUBLANES)
    part = c % chunks_per_group

    def row_of(ref, h, r, cols):
        base = group + (r // V7X_SUBLANES) * V7X_SUBLANES
        x8 = ref[h, pl.ds(base, V7X_SUBLANES), cols]
        r8 = r % V7X_SUBLANES
        row = x8[r8:r8 + 1, :]
        for s in range(1, chunks_per_group):
            o = s * rows_per_chunk + r8
            row = jnp.where(part == s, x8[o:o + 1, :], row)
        return row

    for r in range(rows_per_chunk):
        rows = slice(r * PEER_NKEYS, (r + 1) * PEER_NKEYS)
        for tb in range(tm // V7X_LANES):
            cols = slice(tb * V7X_LANES, (tb + 1) * V7X_LANES)
            w = None
            for h in range(PEER_HEADS):
                tau = row_of(tau_ref, h, r, cols)
                e1 = row_of(e1_ref, h, r, cols)
                wh = jnp.where(s2_ref[h, :, cols] >= tau, e2_ref[h, :, cols] * e1, 0.0)
                w = wh if w is None else w + wh
            aw_ref[rows, cols] = (_gelu_x2(st_ref[rows, cols]) * w).astype(BF16)
    o_ref[...] += lax.dot_general(aw_ref[...], v_ref[...], _CONTRACT_FIRST, preferred_element_type=F32)

    @pl.when(c == pl.num_programs(1) - 1)
    def _():
        o_ref[...] = _layer_norm(DEEPNORM_ALPHA * x_ref[...] + o_ref[...], g_ref[...], b_ref[...])


def _peer_experts(x, u, v, route, g, b, tm, te, cast_tables=(), cast_layer=0):
    m, d = x.shape
    n_exp = u.shape[0]
    n_chunks = n_exp // te
    tok = pl.BlockSpec((tm, d), lambda i, c: (i, 0))
    tab = pl.BlockSpec((te, d), lambda i, c: (c, 0))
    rt = pl.BlockSpec((PEER_HEADS, PEER_NKEYS, tm), lambda i, c: (0, 0, i))
    vec = pl.BlockSpec((1, d), lambda i, c: (0, 0))
    side_in, side_out, side_shapes = _side_cast_specs(cast_tables, cast_layer, (m // tm) * n_chunks,
                                                      lambda i, c: i * n_chunks + c)
    return pl.pallas_call(
        functools.partial(_peer_expert_kernel, te=te, tm=tm, n_side=len(cast_tables)),
        grid=(m // tm, n_chunks),
        in_specs=[tok, tab, tab, rt, rt, rt, rt, vec, vec] + side_in,
        out_specs=[tok] + side_out,
        out_shape=[jax.ShapeDtypeStruct((m, d), F32)] + side_shapes,
        scratch_shapes=[pltpu.VMEM((tm, d), BF16), pltpu.VMEM((te, tm), F32), pltpu.VMEM((te, tm), BF16)],
        compiler_params=_params(2),
        name="peer_experts",
    )(x, u, v, *route, g, b, *cast_tables)


def _peer_layer(x, w_q, sub_keys, u, v, g, b, tm, te, cast_tables=(), cast_layer=0):
    route = _peer_route(x, w_q, sub_keys, tm)
    return _peer_experts(x, u, v, route, g, b, tm, te, cast_tables, cast_layer)


_PROJ_TILE = (512, 1024)
_CONV_IN_TILE = (512, 512)
_LN_ROWS = 256
_PEER_TILE = (512, 1024)


def _row_tile(m, pref):
    return min(m, pref)


def kernel(x_prompt, x_sample, cache_k, cache_v, state_conv, page_table, attn_w_qkv, attn_w_o,
           conv_w_in, conv_w, conv_w_out, ln_mix_g, ln_mix_b, ln_ffn_g, ln_ffn_b,
           peer_w_q, peer_sub_keys, peer_u, peer_v):
    batch, seq, d = x_prompt.shape
    n_dec, t_new, _ = x_sample.shape
    n_past_pages = page_table.shape[1]
    n_past_blocks = n_past_pages // PAGES_PER_BLOCK
    t8 = V7X_SUBLANES

    hp = x_prompt.reshape(batch * seq, d)
    hs = x_sample.reshape(n_dec * t_new, d)
    vec = lambda a: a.reshape(1, d)
    bf = lambda a: a.astype(BF16)

    expert_tables = (peer_u, peer_v)

    def peer(h, layer, u, v, cast_layer=None):
        m = h.shape[0]
        return _peer_layer(h, bf(peer_w_q[layer]), bf(peer_sub_keys[layer]), u, v,
                           vec(ln_ffn_g[layer]), vec(ln_ffn_b[layer]),
                           tm=_row_tile(m, _PEER_TILE[0]), te=_PEER_TILE[1],
                           cast_tables=() if cast_layer is None else expert_tables, cast_layer=cast_layer or 0)

    w_qkv = bf(attn_w_qkv[0])
    w_o = bf(attn_w_o[0])
    qkv_p = _mm(hp, w_qkv, tm=_PROJ_TILE[0], tn=_PROJ_TILE[1])
    qkv_s = _mm(hs, w_qkv, tm=hs.shape[0], tn=_PROJ_TILE[1])
    o_p, new_k_prompt, new_v_prompt, u0, v0 = _moba_prompt(qkv_p, batch, seq, expert_tables, 0)

    heads = lambda a: a.reshape(n_dec, t_new, N_HEADS, HEAD_DIM).transpose(0, 2, 1, 3)
    q_s, k_s, v_s = [heads(a) for a in jnp.split(qkv_s, 3, axis=-1)]
    pad_t = lambda a: jnp.pad(a, ((0, 0), (0, 0), (0, t8 - t_new), (0, 0)))
    assert n_past_pages % _DEC_PAGES_PER_STEP == 0 and n_past_blocks <= V7X_LANES
    k_mean, scores = _dec_k_pass(pad_t(q_s), cache_k, page_table, 0)
    k_mean = jnp.pad(k_mean.transpose(0, 2, 1, 3), ((0, 0), (0, 0), (0, V7X_LANES - n_past_blocks), (0, 0)))
    probs, p_own = _dec_softmax(pad_t(q_s), k_mean, scores, pad_t(k_s), n_past_blocks, t_new)
    o_s = _dec_v_pass(probs, p_own, pad_t(v_s), cache_v, page_table, 0, t_new)
    o_s = o_s[:, :, :t_new].transpose(0, 2, 1, 3).reshape(n_dec * t_new, d)

    hp = _mm_res_ln(o_p, w_o, hp, vec(ln_mix_g[0]), vec(ln_mix_b[0]), tm=_LN_ROWS)
    hs = _mm_res_ln(o_s, w_o, hs, vec(ln_mix_g[0]), vec(ln_mix_b[0]), tm=hs.shape[0])
    hp, u1, v1 = peer(hp, 0, u0, v0, cast_layer=1)
    hs, = peer(hs, 0, u0, v0)

    w_in = bf(conv_w_in[0])
    w_out = bf(conv_w_out[0])
    bg_p, z_p = _conv_in(hp, w_in, tm=_CONV_IN_TILE[0], tn=_CONV_IN_TILE[1])
    bg_s, z_s = _conv_in(hs, w_in, tm=hs.shape[0], tn=_CONV_IN_TILE[1])
    hp = _conv_out_prompt(bg_p, z_p, conv_w[0], w_out, hp, vec(ln_mix_g[1]), vec(ln_mix_b[1]),
                          tm=_LN_ROWS, seq_len=seq)
    time_major = lambda a: a.reshape(n_dec, t_new, d).transpose(1, 0, 2)
    u_s, conv_state_s = _conv_sample(time_major(bg_s), time_major(z_s), state_conv[0].transpose(1, 0, 2), conv_w[0])
    hs = _mm_res_ln(u_s.transpose(1, 0, 2).reshape(n_dec * t_new, d), w_out, hs,
                    vec(ln_mix_g[1]), vec(ln_mix_b[1]), tm=hs.shape[0])
    hp, = peer(hp, 1, u1, v1)
    hs, = peer(hs, 1, u1, v1)

    new_k_sample = k_s[:, None]
    new_v_sample = v_s[:, None]
    new_conv_prompt = z_p.reshape(batch, seq, d)[:, seq - (CONV_WIDTH - 1):][None]
    new_conv_sample = conv_state_s.transpose(1, 0, 2)[None]
    return (hp.reshape(batch, seq, d), hs.reshape(n_dec, t_new, d), new_k_prompt, new_v_prompt,
            new_k_sample, new_v_sample, new_conv_prompt, new_conv_sample)
```

```python
import functools

import jax
import jax.numpy as jnp
from jax import lax
from jax.experimental import pallas as pl
from jax.experimental.pallas import tpu as pltpu

F32 = jnp.float32
BF16 = jnp.bfloat16

D_MODEL = 2048
N_HEADS = 16
HEAD_DIM = 128
PAGE_SIZE = 128
MOBA_BLOCK = 256
MOBA_TOPK = 3
PAGES_PER_BLOCK = MOBA_BLOCK // PAGE_SIZE
CONV_WIDTH = 3
PEER_HEADS = 8
PEER_NKEYS = 128
PEER_TOPK = 16
PEER_HALF = 128
LN_EPS = 1e-5
DEPTH = 2
DEEPNORM_ALPHA = (2.0 * DEPTH) ** 0.25
ATTN_SCALE = HEAD_DIM ** -0.5
LOG2_E = 1.4426950408889634
NEG_INF = float("-inf")

V7X_LANES = 128
V7X_SUBLANES = 8
V7X_VMEM_LIMIT_BYTES = 60 * 1024 * 1024

_CONTRACT_LAST = (((1,), (1,)), ((), ()))
_CONTRACT_FIRST = (((0,), (0,)), ((), ()))
_BATCH_CONTRACT_LAST = (((2,), (2,)), ((0,), (0,)))
_BATCH_MATMUL = (((2,), (1,)), ((0,), (0,)))


def _params(n_axes):
    return pltpu.CompilerParams(dimension_semantics=("arbitrary",) * n_axes,
                                vmem_limit_bytes=V7X_VMEM_LIMIT_BYTES)


def _dot(a, b):
    return jnp.dot(a, b, preferred_element_type=F32)


def _dot_nt(a, b):
    return lax.dot_general(a, b, _CONTRACT_LAST, preferred_element_type=F32)


def _split_bf16(x):
    hi = x.astype(BF16)
    lo = (x - hi.astype(F32)).astype(BF16)
    return hi, lo


def _layer_norm(x, g, b):
    mu = jnp.mean(x, axis=-1, keepdims=True)
    xc = x - mu
    var = jnp.mean(xc * xc, axis=-1, keepdims=True)
    return xc * lax.rsqrt(var + LN_EPS) * g + b


def _mm_kernel(x_ref, w_ref, o_ref):
    o_ref[...] = _dot(x_ref[...].astype(BF16), w_ref[...])


def _mm(x, w, tm, tn):
    m, k = x.shape
    n = w.shape[1]
    return pl.pallas_call(
        _mm_kernel,
        grid=(n // tn, m // tm),
        in_specs=[pl.BlockSpec((tm, k), lambda j, i: (i, 0)),
                  pl.BlockSpec((k, tn), lambda j, i: (0, j))],
        out_specs=pl.BlockSpec((tm, tn), lambda j, i: (i, j)),
        out_shape=jax.ShapeDtypeStruct((m, n), F32),
        compiler_params=_params(2),
        name="proj",
    )(x, w)


def _mm_res_ln_kernel(x_ref, w_ref, h_ref, g_ref, b_ref, o_ref):
    y = _dot(x_ref[...].astype(BF16), w_ref[...])
    o_ref[...] = _layer_norm(DEEPNORM_ALPHA * h_ref[...] + y, g_ref[...], b_ref[...])


def _mm_res_ln(x, w, h, g, b, tm):
    m, k = x.shape
    d = w.shape[1]
    row = lambda i: (i, 0)
    fixed = lambda i: (0, 0)
    return pl.pallas_call(
        _mm_res_ln_kernel,
        grid=(m // tm,),
        in_specs=[pl.BlockSpec((tm, k), row), pl.BlockSpec((k, d), fixed),
                  pl.BlockSpec((tm, d), row), pl.BlockSpec((1, d), fixed), pl.BlockSpec((1, d), fixed)],
        out_specs=pl.BlockSpec((tm, d), row),
        out_shape=jax.ShapeDtypeStruct((m, d), F32),
        compiler_params=_params(1),
        name="out_proj_ln",
    )(x, w, h, g, b)


def _conv_in_kernel(x_ref, wb_ref, wc_ref, wh_ref, bg_ref, z_ref):
    xb = x_ref[...].astype(BF16)
    bg_ref[...] = _dot(xb, wb_ref[...])
    z_ref[...] = _dot(xb, wc_ref[...]) * _dot(xb, wh_ref[...])


def _conv_in(x, w_in, tm, tn):
    m, k = x.shape
    d = w_in.shape[1] // 3
    nb = d // tn
    xs = pl.BlockSpec((tm, k), lambda j, i: (i, 0))
    ws = [pl.BlockSpec((k, tn), functools.partial(lambda j, i, off: (0, j + off), off=part * nb))
          for part in range(3)]
    os_ = pl.BlockSpec((tm, tn), lambda j, i: (i, j))
    return pl.pallas_call(
        _conv_in_kernel,
        grid=(nb, m // tm),
        in_specs=[xs] + ws,
        out_specs=[os_, os_],
        out_shape=[jax.ShapeDtypeStruct((m, d), F32)] * 2,
        compiler_params=_params(2),
        name="conv_in",
    )(x, w_in, w_in, w_in)


def _conv_out_prompt_kernel(bg_ref, z_ref, zp_ref, cw_ref, w_ref, h_ref, g_ref, b_ref, o_ref, *, tiles_per_seq):
    i = pl.program_id(0)
    z = z_ref[...]
    seq_start = (i % tiles_per_seq) == 0
    zp = jnp.where(seq_start, 0.0, zp_ref[...])
    row = lax.broadcasted_iota(jnp.int32, z.shape, 0)
    last = V7X_SUBLANES - 1
    z1 = jnp.where(row == 0, zp[last:last + 1, :], pltpu.roll(z, 1, 0))
    z2 = jnp.where(row == 0, zp[last - 1:last, :],
                   jnp.where(row == 1, zp[last:last + 1, :], pltpu.roll(z, 2, 0)))
    cw = cw_ref[...]
    y = cw[0:1, :] * z2 + cw[1:2, :] * z1 + cw[2:3, :] * z
    u = (bg_ref[...] * y).astype(BF16)
    o_ref[...] = _layer_norm(DEEPNORM_ALPHA * h_ref[...] + _dot(u, w_ref[...]), g_ref[...], b_ref[...])


def _conv_out_prompt(bg, z, conv_w, w_out, h, g, b, tm, seq_len):
    m, d = z.shape
    row = lambda i: (i, 0)
    fixed = lambda i: (0, 0)
    halo = lambda i: (jnp.maximum(i * (tm // V7X_SUBLANES) - 1, 0), 0)
    return pl.pallas_call(
        functools.partial(_conv_out_prompt_kernel, tiles_per_seq=seq_len // tm),
        grid=(m // tm,),
        in_specs=[pl.BlockSpec((tm, d), row), pl.BlockSpec((tm, d), row),
                  pl.BlockSpec((V7X_SUBLANES, d), halo), pl.BlockSpec((CONV_WIDTH, d), fixed),
                  pl.BlockSpec((d, d), fixed), pl.BlockSpec((tm, d), row),
                  pl.BlockSpec((1, d), fixed), pl.BlockSpec((1, d), fixed)],
        out_specs=pl.BlockSpec((tm, d), row),
        out_shape=jax.ShapeDtypeStruct((m, d), F32),
        compiler_params=_params(1),
        name="conv_out_prompt",
    )(bg, z, z, conv_w, w_out, h, g, b)


def _conv_sample_kernel(bg_ref, z_ref, st_ref, cw_ref, u_ref, ns_ref, *, t):
    cw = cw_ref[...]
    zp = [st_ref[j] for j in range(CONV_WIDTH - 1)] + [z_ref[j] for j in range(t)]
    for j in range(t):
        y = cw[0:1, :] * zp[j] + cw[1:2, :] * zp[j + 1] + cw[2:3, :] * zp[j + 2]
        u_ref[j] = bg_ref[j] * y
    for j in range(CONV_WIDTH - 1):
        ns_ref[j] = zp[t + j]


def _conv_sample(bg, z, state, conv_w):
    t, n, d = z.shape
    return pl.pallas_call(
        functools.partial(_conv_sample_kernel, t=t),
        out_shape=[jax.ShapeDtypeStruct((t, n, d), F32), jax.ShapeDtypeStruct((CONV_WIDTH - 1, n, d), F32)],
        name="conv_sample",
    )(bg, z, state, conv_w)


def _topk_mask(g, n_valid, lane, n_cand):
    rank = jnp.zeros(g.shape, F32)
    for c in range(n_cand):
        col = g[..., c:c + 1]
        beats = (col > g) | ((col == g) & (c < lane))
        rank = rank + jnp.where(beats & (c < n_valid), 1.0, 0.0)
    return (lane < n_valid) & (rank < MOBA_TOPK)


def _side_cast_specs(tables, layer, n_steps, step_index):
    in_specs, out_specs, out_shapes = [], [], []
    for t in tables:
        _, rows, cols = t.shape
        slab = rows // n_steps
        assert slab * n_steps == rows and slab % (2 * V7X_SUBLANES) == 0
        in_specs.append(pl.BlockSpec((None, slab, cols), lambda *g: (layer, step_index(*g), 0)))
        out_specs.append(pl.BlockSpec((slab, cols), lambda *g: (step_index(*g), 0)))
        out_shapes.append(jax.ShapeDtypeStruct((rows, cols), BF16))
    return in_specs, out_specs, out_shapes


def _side_cast(src_refs, dst_refs):
    for src, dst in zip(src_refs, dst_refs):
        dst[...] = src[...].astype(dst.dtype)


def _moba_prompt_kernel(q_ref, k_ref, v_ref, *refs, n_blk, n_side):
    side_in, (o_ref, kp_ref, vp_ref) = refs[:n_side], refs[n_side:n_side + 3]
    side_out, km_ref = refs[n_side + 3:2 * n_side + 3], refs[2 * n_side + 3]
    _side_cast(side_in, side_out)
    blk = MOBA_BLOCK
    k = k_ref[...]
    v = v_ref[...]
    kp_ref[...] = k.reshape(kp_ref.shape)
    vp_ref[...] = v.reshape(vp_ref.shape)
    k_bf = k.astype(BF16)
    vt_bf = v.T.astype(BF16)
    km_ref[...] = jnp.zeros(km_ref.shape, F32)
    for n in range(n_blk):
        km_ref[n:n + 1, :] = jnp.mean(k[n * blk:(n + 1) * blk, :], axis=0, keepdims=True)
    q_hi, q_lo = _split_bf16(q_ref[...])
    km_hi, km_lo = _split_bf16(km_ref[...])
    gate_t = _dot_nt(km_hi, q_hi) + _dot_nt(km_hi, q_lo) + _dot_nt(km_lo, q_hi)
    blk_row = lax.broadcasted_iota(jnp.int32, (km_ref.shape[0], blk), 0)
    key_id = lax.broadcasted_iota(jnp.int32, (blk, blk), 0)
    qry_id = lax.broadcasted_iota(jnp.int32, (blk, blk), 1)

    def selected(qb):
        g = gate_t[:, qb * blk:(qb + 1) * blk]
        rank = jnp.zeros(g.shape, F32)
        for c in range(qb):
            gc = g[c:c + 1, :]
            beats = (gc > g) | ((gc == g) & (c < blk_row))
            rank = rank + jnp.where(beats, 1.0, 0.0)
        return jnp.where((blk_row < qb) & (rank < MOBA_TOPK), 1.0, 0.0)

    pair = 2
    for qp in range(n_blk // pair):
        qbs = [pair * qp + a for a in range(pair)]
        qs = slice(qbs[0] * blk, (qbs[-1] + 1) * blk)
        n_keys = (qbs[-1] + 1) * blk
        s = _dot_nt(k_bf[:n_keys], q_hi[qs]) * (ATTN_SCALE * LOG2_E)
        sel = [selected(qb) if qb > MOBA_TOPK else None for qb in qbs]
        pieces = []
        for n in range(qbs[-1] + 1):
            halves = []
            for a, qb in enumerate(qbs):
                sn = s[n * blk:(n + 1) * blk, a * blk:(a + 1) * blk]
                if n == qb:
                    sn = jnp.where(key_id <= qry_id, sn, NEG_INF)
                elif n > qb:
                    sn = jnp.full(sn.shape, NEG_INF, F32)
                elif sel[a] is not None:
                    sn = jnp.where(sel[a][n:n + 1, :] > 0.5, sn, NEG_INF)
                halves.append(sn)
            pieces.append(jnp.concatenate(halves, axis=1))
        m = jnp.max(pieces[0], axis=0, keepdims=True)
        for sn in pieces[1:]:
            m = jnp.maximum(m, jnp.max(sn, axis=0, keepdims=True))
        p = [jnp.exp2(sn - m) for sn in pieces]
        l = jnp.sum(p[0], axis=0, keepdims=True)
        for pn in p[1:]:
            l = l + jnp.sum(pn, axis=0, keepdims=True)
        p_bf = jnp.concatenate([pn.astype(BF16) for pn in p], axis=0)
        out_t = _dot(vt_bf[:, :n_keys], p_bf) * (1.0 / l)
        o_ref[qs, :] = out_t.T.astype(o_ref.dtype)


def _moba_prompt(qkv, batch, seq, cast_tables=(), cast_layer=0):
    n_blk = seq // MOBA_BLOCK
    assert n_blk * MOBA_BLOCK == seq and n_blk % 2 == 0
    n_pages = seq // PAGE_SIZE
    hd = HEAD_DIM
    km_rows = -(-n_blk // V7X_SUBLANES) * V7X_SUBLANES
    col = lambda part: pl.BlockSpec((seq, hd), lambda b, h: (b, part * N_HEADS + h))
    page_spec = pl.BlockSpec((None, n_pages, None, None, PAGE_SIZE, hd), lambda b, h: (b, 0, 0, h, 0, 0))
    page_shape = jax.ShapeDtypeStruct((batch, n_pages, 1, N_HEADS, PAGE_SIZE, hd), F32)
    side_in, side_out, side_shapes = _side_cast_specs(cast_tables, cast_layer, batch * N_HEADS,
                                                      lambda b, h: b * N_HEADS + h)
    return pl.pallas_call(
        functools.partial(_moba_prompt_kernel, n_blk=n_blk, n_side=len(cast_tables)),
        grid=(batch, N_HEADS),
        in_specs=[col(0), col(1), col(2)] + side_in,
        out_specs=[pl.BlockSpec((seq, hd), lambda b, h: (b, h)), page_spec, page_spec] + side_out,
        out_shape=[jax.ShapeDtypeStruct((batch * seq, D_MODEL), BF16), page_shape, page_shape] + side_shapes,
        scratch_shapes=[pltpu.VMEM((km_rows, hd), F32)],
        compiler_params=_params(2),
        name="moba_prompt",
    )(qkv, qkv, qkv, *cast_tables)


_BATCH_QK = functools.partial(lax.dot_general, dimension_numbers=_BATCH_CONTRACT_LAST, preferred_element_type=F32)
_BATCH_PV = functools.partial(lax.dot_general, dimension_numbers=_BATCH_MATMUL, preferred_element_type=F32)

_DEC_PAGES_PER_STEP = 8 * PAGES_PER_BLOCK


def _page_specs(n_heads, layer):
    def spec(which):
        return pl.BlockSpec((None, None, n_heads, PAGE_SIZE, HEAD_DIM),
                            lambda i, j, pt: (pt[i, _DEC_PAGES_PER_STEP * j + which], layer, 0, 0, 0))
    return [spec(w) for w in range(_DEC_PAGES_PER_STEP)]


def _dec_k_pass_kernel(pt_ref, q_ref, *refs):
    del pt_ref
    k_refs = refs[:_DEC_PAGES_PER_STEP]
    km_ref, s_ref = refs[_DEC_PAGES_PER_STEP:]
    q_bf = q_ref[...].astype(BF16)
    sums = []
    for pg, k_ref in enumerate(k_refs):
        k = k_ref[...]
        sums.append(jnp.sum(k, axis=1))
        s_ref[:, :, pg * PAGE_SIZE:(pg + 1) * PAGE_SIZE] = _BATCH_QK(q_bf, k.astype(BF16)) * ATTN_SCALE
    for b in range(_DEC_PAGES_PER_STEP // PAGES_PER_BLOCK):
        tot = sums[b * PAGES_PER_BLOCK]
        for pg in range(1, PAGES_PER_BLOCK):
            tot = tot + sums[b * PAGES_PER_BLOCK + pg]
        km_ref[b] = tot / MOBA_BLOCK


def _dec_k_pass(q, cache_k, page_table, layer):
    n, h, t8, hd = q.shape
    n_pages = page_table.shape[1]
    n_blocks = n_pages // PAGES_PER_BLOCK
    blocks_per_step = _DEC_PAGES_PER_STEP // PAGES_PER_BLOCK
    keys_per_step = _DEC_PAGES_PER_STEP * PAGE_SIZE
    return pl.pallas_call(
        _dec_k_pass_kernel,
        grid_spec=pltpu.PrefetchScalarGridSpec(
            num_scalar_prefetch=1, grid=(n, n_pages // _DEC_PAGES_PER_STEP),
            in_specs=[pl.BlockSpec((None, h, t8, hd), lambda i, j, pt: (i, 0, 0, 0))] + _page_specs(h, layer),
            out_specs=[pl.BlockSpec((None, blocks_per_step, h, hd), lambda i, j, pt: (i, j, 0, 0)),
                       pl.BlockSpec((None, h, t8, keys_per_step), lambda i, j, pt: (i, 0, 0, j))]),
        out_shape=[jax.ShapeDtypeStruct((n, n_blocks, h, hd), F32),
                   jax.ShapeDtypeStruct((n, h, t8, n_pages * PAGE_SIZE), F32)],
        compiler_params=_params(2),
        name="dec_k_pass",
    )(page_table, q, *([cache_k] * _DEC_PAGES_PER_STEP))


def _dec_softmax_kernel(q_ref, km_ref, s_ref, kn_ref, p_ref, pown_ref, *, n_blocks, t_new):
    q_hi, q_lo = _split_bf16(q_ref[...])
    km_hi, km_lo = _split_bf16(km_ref[...])
    gate = _BATCH_QK(q_hi, km_hi) + _BATCH_QK(q_lo, km_hi) + _BATCH_QK(q_hi, km_lo)
    lane = lax.broadcasted_iota(jnp.int32, gate.shape, 2)
    sel = jnp.where(_topk_mask(gate, n_blocks, lane, n_blocks), 1.0, 0.0)

    t_id = lax.broadcasted_iota(jnp.int32, (gate.shape[0], gate.shape[1], 1), 1)
    qf = q_hi.astype(F32)
    kf = kn_ref[...].astype(BF16).astype(F32)
    s_own = [jnp.where(c <= t_id, jnp.sum(qf * kf[:, c:c + 1, :], axis=-1, keepdims=True) * ATTN_SCALE, NEG_INF)
             for c in range(t_new)]
    m = s_own[0]
    for c in range(1, t_new):
        m = jnp.maximum(m, s_own[c])
    pieces = []
    for b in range(n_blocks):
        sb = jnp.where(sel[:, :, b:b + 1] > 0.5, s_ref[:, :, b * MOBA_BLOCK:(b + 1) * MOBA_BLOCK], NEG_INF)
        pieces.append(sb)
        m = jnp.maximum(m, jnp.max(sb, axis=-1, keepdims=True))
    e_own = [jnp.exp(s - m) for s in s_own]
    l = e_own[0]
    for e in e_own[1:]:
        l = l + e
    e_past = [jnp.exp(sb - m) for sb in pieces]
    for e in e_past:
        l = l + jnp.sum(e, axis=-1, keepdims=True)
    for b, e in enumerate(e_past):
        p_ref[:, :, b * MOBA_BLOCK:(b + 1) * MOBA_BLOCK] = e / l
    own = jnp.zeros(pown_ref.shape, F32)
    own_lane = lax.broadcasted_iota(jnp.int32, pown_ref.shape, 2)
    for c in range(t_new):
        own = jnp.where(own_lane == c, e_own[c] / l, own)
    pown_ref[...] = own


def _dec_softmax(q, k_mean, scores, k_new, n_blocks, t_new):
    n, h, t8, hd = q.shape
    past = scores.shape[-1]
    spec = lambda rows, cols: pl.BlockSpec((None, h, rows, cols), lambda i: (i, 0, 0, 0))
    return pl.pallas_call(
        functools.partial(_dec_softmax_kernel, n_blocks=n_blocks, t_new=t_new),
        grid=(n,),
        in_specs=[spec(t8, hd), spec(V7X_LANES, hd), spec(t8, past), spec(t8, hd)],
        out_specs=[spec(t8, past), spec(t8, V7X_LANES)],
        out_shape=[jax.ShapeDtypeStruct((n, h, t8, past), F32), jax.ShapeDtypeStruct((n, h, t8, V7X_LANES), F32)],
        compiler_params=_params(1),
        name="dec_softmax",
    )(q, k_mean, scores, k_new)


def _dec_v_pass_kernel(pt_ref, p_ref, pown_ref, vn_ref, *refs, t_new):
    del pt_ref
    v_refs = refs[:_DEC_PAGES_PER_STEP]
    o_ref = refs[_DEC_PAGES_PER_STEP]

    @pl.when(pl.program_id(1) == 0)
    def _():
        p_own = pown_ref[...].astype(BF16).astype(F32)
        vf = vn_ref[...].astype(BF16).astype(F32)
        acc = jnp.zeros(o_ref.shape, F32)
        for c in range(t_new):
            acc = acc + p_own[:, :, c:c + 1] * vf[:, c:c + 1, :]
        o_ref[...] = acc

    tot = o_ref[...]
    for pg, v_ref in enumerate(v_refs):
        p = p_ref[:, :, pg * PAGE_SIZE:(pg + 1) * PAGE_SIZE].astype(BF16)
        tot = tot + _BATCH_PV(p, v_ref[...].astype(BF16))
    o_ref[...] = tot


def _dec_v_pass(probs, p_own, v_new, cache_v, page_table, layer, t_new):
    n, h, t8, hd = v_new.shape
    n_pages = page_table.shape[1]
    keys_per_step = _DEC_PAGES_PER_STEP * PAGE_SIZE
    tok = lambda cols: pl.BlockSpec((None, h, t8, cols), lambda i, j, pt: (i, 0, 0, 0))
    return pl.pallas_call(
        functools.partial(_dec_v_pass_kernel, t_new=t_new),
        grid_spec=pltpu.PrefetchScalarGridSpec(
            num_scalar_prefetch=1, grid=(n, n_pages // _DEC_PAGES_PER_STEP),
            in_specs=[pl.BlockSpec((None, h, t8, keys_per_step), lambda i, j, pt: (i, 0, 0, j)),
                      tok(V7X_LANES), tok(hd)] + _page_specs(h, layer),
            out_specs=tok(hd)),
        out_shape=jax.ShapeDtypeStruct((n, h, t8, hd), F32),
        compiler_params=_params(2),
        name="dec_v_pass",
    )(page_table, probs, p_own, v_new, *([cache_v] * _DEC_PAGES_PER_STEP))


def _top_values(s, k):
    vals = []
    for r in range(k):
        m = jnp.max(s, axis=0, keepdims=True)
        vals.append(m)
        if r + 1 < k:
            s = jnp.where(s == m, NEG_INF, s)
    return vals


def _batcher_sort_network(lo, hi):
    def merge(lo, hi, r):
        step = 2 * r
        if step < hi - lo:
            yield from merge(lo, hi, step)
            yield from merge(lo + r, hi, step)
            yield from ((i, i + r) for i in range(lo + r, hi - r, step))
        else:
            yield (lo, lo + r)
    if hi - lo >= 1:
        mid = lo + (hi - lo) // 2
        yield from _batcher_sort_network(lo, mid)
        yield from _batcher_sort_network(mid + 1, hi)
        yield from merge(lo, hi, 1)


def _compare_exchange(rows, i, j):
    rows[i], rows[j] = jnp.maximum(rows[i], rows[j]), jnp.minimum(rows[i], rows[j])


def _top_values_network(s):
    sub = V7X_SUBLANES
    n = s.shape[0] // sub
    assert n * sub == s.shape[0] and n & (n - 1) == 0
    slabs = [s[v * sub:(v + 1) * sub, :] for v in range(n)]
    for i, j in _batcher_sort_network(0, n - 1):
        _compare_exchange(slabs, i, j)
    shift = sub // 2
    while shift >= 1:
        partner = [pltpu.roll(x, shift, 0) for x in slabs]
        slabs = [jnp.maximum(slabs[v], partner[n - 1 - v]) for v in range(n)]
        d = n // 2
        while d >= 1:
            for v in range(n):
                if v & d == 0:
                    _compare_exchange(slabs, v, v + d)
            d //= 2
        shift //= 2
    return [x[0:1, :] for x in slabs]


_PEER_CANDIDATES = [(a, b) for a in range(PEER_TOPK) for b in range(PEER_TOPK // (a + 1))]
_PEER_CAND_ROWS = -(-len(_PEER_CANDIDATES) // V7X_SUBLANES) * V7X_SUBLANES


def _peer_route_kernel(x_ref, wq_ref, sk_ref, s2_ref, e2_ref, tau_ref, e1_ref, cand_ref):
    q = _dot(x_ref[...].astype(BF16), wq_ref[...]).astype(BF16)
    cand_ref[...] = jnp.full(cand_ref.shape, NEG_INF, F32)
    for h in range(PEER_HEADS):
        s = []
        for p in range(2):
            c0 = (2 * h + p) * PEER_HALF
            s.append(_dot_nt(sk_ref[h, p], q[:, c0:c0 + PEER_HALF]))
        v1 = _top_values_network(s[0])
        v2 = _top_values_network(s[1])
        assert len(v1) == len(v2) == PEER_TOPK
        next1, next2 = [jnp.max(jnp.where(s[p] < v[-1], s[p], NEG_INF), axis=0, keepdims=True)
                        for p, v in enumerate((v1, v2))]
        sums = [v1[a] + v2[b] for a, b in _PEER_CANDIDATES] + [v1[0] + next2, next1 + v2[0]]
        for r, c in enumerate(sums):
            cand_ref[r:r + 1, :] = c
        cand = cand_ref[...]
        top = _top_values(cand, PEER_TOPK + 1)
        thr, runner_up = top[-2], top[-1]
        z = jnp.sum(jnp.where(cand >= thr, jnp.exp(cand - (v1[0] + v2[0])), 0.0), axis=0, keepdims=True)
        s2_ref[h] = s[1]
        e2_ref[h] = jnp.exp(s[1] - v2[0])
        tau_ref[h] = 0.5 * (thr + runner_up) - s[0]
        e1_ref[h] = 0.5 * jnp.exp(s[0] - v1[0]) / z


def _peer_route(x, w_q, sub_keys, tm):
    m, d = x.shape
    out_spec = pl.BlockSpec((PEER_HEADS, PEER_NKEYS, tm), lambda i: (0, 0, i))
    out_shape = jax.ShapeDtypeStruct((PEER_HEADS, PEER_NKEYS, m), F32)
    return pl.pallas_call(
        _peer_route_kernel,
        grid=(m // tm,),
        in_specs=[pl.BlockSpec((tm, d), lambda i: (i, 0)),
                  pl.BlockSpec(w_q.shape, lambda i: (0, 0)),
                  pl.BlockSpec(sub_keys.shape, lambda i: (0, 0, 0, 0))],
        out_specs=[out_spec] * 4,
        out_shape=[out_shape] * 4,
        scratch_shapes=[pltpu.VMEM((_PEER_CAND_ROWS, tm), F32)],
        compiler_params=_params(1),
        name="peer_route",
    )(x, w_q, sub_keys)


def _gelu_x2(x):
    return x * (1.0 + lax.erf(x * (2.0 ** -0.5)))


def _peer_expert_kernel(x_ref, u_ref, v_ref, s2_ref, e2_ref, tau_ref, e1_ref, g_ref, b_ref,
                        *refs, te, tm, n_side):
    side_in, o_ref, side_out = refs[:n_side], refs[n_side], refs[n_side + 1:2 * n_side + 1]
    xb_ref, st_ref, aw_ref = refs[2 * n_side + 1:]
    _side_cast(side_in, side_out)
    c = pl.program_id(1)

    @pl.when(c == 0)
    def _():
        o_ref[...] = jnp.zeros(o_ref.shape, F32)
        xb_ref[...] = x_ref[...].astype(BF16)

    st_ref[...] = _dot_nt(u_ref[...], xb_ref[...])

    rows_per_chunk = te // PEER_NKEYS
    chunks_per_group = max(V7X_SUBLANES // rows_per_chunk, 1)
    group = pl.multiple_of((c * rows_per_chunk // V7X_SUBLANES) * V7X_SUBLANES, V7X_SUBLANES)
    part = c % chunks_per_group

    def row_of(ref, h, r, cols):
        base = group + (r // V7X_SUBLANES) * V7X_SUBLANES
        x8 = ref[h, pl.ds(base, V7X_SUBLANES), cols]
        r8 = r % V7X_SUBLANES
        row = x8[r8:r8 + 1, :]
        for s in range(1, chunks_per_group):
            o = s * rows_per_chunk + r8
            row = jnp.where(part == s, x8[o:o + 1, :], row)
        return row

    for r in range(rows_per_chunk):
        rows = slice(r * PEER_NKEYS, (r + 1) * PEER_NKEYS)
        for tb in range(tm // V7X_LANES):
            cols = slice(tb * V7X_LANES, (tb + 1) * V7X_LANES)
            w = None
            for h in range(PEER_HEADS):
                tau = row_of(tau_ref, h, r, cols)
                e1 = row_of(e1_ref, h, r, cols)
                wh = jnp.where(s2_ref[h, :, cols] >= tau, e2_ref[h, :, cols] * e1, 0.0)
                w = wh if w is None else w + wh
            aw_ref[rows, cols] = (_gelu_x2(st_ref[rows, cols]) * w).astype(BF16)
    o_ref[...] += lax.dot_general(aw_ref[...], v_ref[...], _CONTRACT_FIRST, preferred_element_type=F32)

    @pl.when(c == pl.num_programs(1) - 1)
    def _():
        o_ref[...] = _layer_norm(DEEPNORM_ALPHA * x_ref[...] + o_ref[...], g_ref[...], b_ref[...])


def _peer_experts(x, u, v, route, g, b, tm, te, cast_tables=(), cast_layer=0):
    m, d = x.shape
    n_exp = u.shape[0]
    n_chunks = n_exp // te
    tok = pl.BlockSpec((tm, d), lambda i, c: (i, 0))
    tab = pl.BlockSpec((te, d), lambda i, c: (c, 0))
    rt = pl.BlockSpec((PEER_HEADS, PEER_NKEYS, tm), lambda i, c: (0, 0, i))
    vec = pl.BlockSpec((1, d), lambda i, c: (0, 0))
    side_in, side_out, side_shapes = _side_cast_specs(cast_tables, cast_layer, (m // tm) * n_chunks,
                                                      lambda i, c: i * n_chunks + c)
    return pl.pallas_call(
        functools.partial(_peer_expert_kernel, te=te, tm=tm, n_side=len(cast_tables)),
        grid=(m // tm, n_chunks),
        in_specs=[tok, tab, tab, rt, rt, rt, rt, vec, vec] + side_in,
        out_specs=[tok] + side_out,
        out_shape=[jax.ShapeDtypeStruct((m, d), F32)] + side_shapes,
        scratch_shapes=[pltpu.VMEM((tm, d), BF16), pltpu.VMEM((te, tm), F32), pltpu.VMEM((te, tm), BF16)],
        compiler_params=_params(2),
        name="peer_experts",
    )(x, u, v, *route, g, b, *cast_tables)


def _peer_layer(x, w_q, sub_keys, u, v, g, b, tm, te, cast_tables=(), cast_layer=0):
    route = _peer_route(x, w_q, sub_keys, tm)
    return _peer_experts(x, u, v, route, g, b, tm, te, cast_tables, cast_layer)


_PROJ_TILE = (512, 1024)
_CONV_IN_TILE = (512, 512)
_LN_ROWS = 256
_PEER_TILE = (512, 1024)


def _row_tile(m, pref):
    return min(m, pref)


def kernel(x_prompt, x_sample, cache_k, cache_v, state_conv, page_table, attn_w_qkv, attn_w_o,
           conv_w_in, conv_w, conv_w_out, ln_mix_g, ln_mix_b, ln_ffn_g, ln_ffn_b,
           peer_w_q, peer_sub_keys, peer_u, peer_v):
    batch, seq, d = x_prompt.shape
    n_dec, t_new, _ = x_sample.shape
    n_past_pages = page_table.shape[1]
    n_past_blocks = n_past_pages // PAGES_PER_BLOCK
    t8 = V7X_SUBLANES

    hp = x_prompt.reshape(batch * seq, d)
    hs = x_sample.reshape(n_dec * t_new, d)
    vec = lambda a: a.reshape(1, d)
    bf = lambda a: a.astype(BF16)

    expert_tables = (peer_u, peer_v)

    def peer(h, layer, u, v, cast_layer=None):
        m = h.shape[0]
        return _peer_layer(h, bf(peer_w_q[layer]), bf(peer_sub_keys[layer]), u, v,
                           vec(ln_ffn_g[layer]), vec(ln_ffn_b[layer]),
                           tm=_row_tile(m, _PEER_TILE[0]), te=_PEER_TILE[1],
                           cast_tables=() if cast_layer is None else expert_tables, cast_layer=cast_layer or 0)

    w_qkv = bf(attn_w_qkv[0])
    w_o = bf(attn_w_o[0])
    qkv_p = _mm(hp, w_qkv, tm=_PROJ_TILE[0], tn=_PROJ_TILE[1])
    qkv_s = _mm(hs, w_qkv, tm=hs.shape[0], tn=_PROJ_TILE[1])
    o_p, new_k_prompt, new_v_prompt, u0, v0 = _moba_prompt(qkv_p, batch, seq, expert_tables, 0)

    heads = lambda a: a.reshape(n_dec, t_new, N_HEADS, HEAD_DIM).transpose(0, 2, 1, 3)
    q_s, k_s, v_s = [heads(a) for a in jnp.split(qkv_s, 3, axis=-1)]
    pad_t = lambda a: jnp.pad(a, ((0, 0), (0, 0), (0, t8 - t_new), (0, 0)))
    assert n_past_pages % _DEC_PAGES_PER_STEP == 0 and n_past_blocks <= V7X_LANES
    k_mean, scores = _dec_k_pass(pad_t(q_s), cache_k, page_table, 0)
    k_mean = jnp.pad(k_mean.transpose(0, 2, 1, 3), ((0, 0), (0, 0), (0, V7X_LANES - n_past_blocks), (0, 0)))
    probs, p_own = _dec_softmax(pad_t(q_s), k_mean, scores, pad_t(k_s), n_past_blocks, t_new)
    o_s = _dec_v_pass(probs, p_own, pad_t(v_s), cache_v, page_table, 0, t_new)
    o_s = o_s[:, :, :t_new].transpose(0, 2, 1, 3).reshape(n_dec * t_new, d)

    hp = _mm_res_ln(o_p, w_o, hp, vec(ln_mix_g[0]), vec(ln_mix_b[0]), tm=_LN_ROWS)
    hs = _mm_res_ln(o_s, w_o, hs, vec(ln_mix_g[0]), vec(ln_mix_b[0]), tm=hs.shape[0])
    hp, u1, v1 = peer(hp, 0, u0, v0, cast_layer=1)
    hs, = peer(hs, 0, u0, v0)

    w_in = bf(conv_w_in[0])
    w_out = bf(conv_w_out[0])
    bg_p, z_p = _conv_in(hp, w_in, tm=_CONV_IN_TILE[0], tn=_CONV_IN_TILE[1])
    bg_s, z_s = _conv_in(hs, w_in, tm=hs.shape[0], tn=_CONV_IN_TILE[1])
    hp = _conv_out_prompt(bg_p, z_p, conv_w[0], w_out, hp, vec(ln_mix_g[1]), vec(ln_mix_b[1]),
                          tm=_LN_ROWS, seq_len=seq)
    time_major = lambda a: a.reshape(n_dec, t_new, d).transpose(1, 0, 2)
    u_s, conv_state_s = _conv_sample(time_major(bg_s), time_major(z_s), state_conv[0].transpose(1, 0, 2), conv_w[0])
    hs = _mm_res_ln(u_s.transpose(1, 0, 2).reshape(n_dec * t_new, d), w_out, hs,
                    vec(ln_mix_g[1]), vec(ln_mix_b[1]), tm=hs.shape[0])
    hp, = peer(hp, 1, u1, v1)
    hs, = peer(hs, 1, u1, v1)

    new_k_sample = k_s[:, None]
    new_v_sample = v_s[:, None]
    new_conv_prompt = z_p.reshape(batch, seq, d)[:, seq - (CONV_WIDTH - 1):][None]
    new_conv_sample = conv_state_s.transpose(1, 0, 2)[None]
    return (hp.reshape(batch, seq, d), hs.reshape(n_dec, t_new, d), new_k_prompt, new_v_prompt,
            new_k_sample, new_v_sample, new_conv_prompt, new_conv_sample)
```
